```python
import math
import jax, jax.numpy as jnp
from jax import lax
import numpy as np

D_MODEL = 1024
BATCH = 8
SEQ = 8192
DEPTH = 2
DEC_BATCH = 32
DEC_SEQ = 2048
PAST_LEN = 128

N_BRANCH = 4
BR_W = 256
A_HEADS = 4
A_DK = 32
A_DV = 2 * A_DK
B_HEADS = 4
B_DK = 64
B_DV = 64
C_HEADS = 4
C_NOPE = 32
C_ROPE = 16
C_V = 64
C_Q_LORA = 192
C_KV_LORA = 128
ROPE_BASE = 10000.0
D_BLOCKS = 4
D_BLOCK_W = BR_W // D_BLOCKS
CONV_W = 4
CONV_LEFT = 2
RG_C = 8.0
N_EXPERTS = 16
D_EXPERT = 2048
CAPACITY_FACTOR = 2
Q_BLOCK = 128
CHUNK = 128
LN_EPS = 1e-5
RMS_EPS = 1e-6
ALPHA = (2 * DEPTH) ** 0.25
BETA = (8 * DEPTH) ** -0.25
ALIBI_SLOPES = tuple(2.0 ** (-8.0 * (h + 1) / A_HEADS) for h in range(A_HEADS))

_IN_WIDTHS = (
    A_HEADS * 2 * A_DK,
    A_HEADS * 2 * A_DK,
    A_HEADS * A_DV,
    B_HEADS * B_DK,
    B_HEADS * B_DK,
    B_HEADS * B_DK,
    B_HEADS * B_DV,
    B_HEADS * B_DV,
    C_Q_LORA,
    C_KV_LORA,
    C_ROPE,
    BR_W,
    BR_W,
    N_BRANCH * D_MODEL,
)
IN_W = sum(_IN_WIDTHS)
IN_SPLITS = tuple(int(s) for s in np.cumsum(_IN_WIDTHS)[:-1])

kernel_name = 'hybrid_bidir_encoder_gated_merge'


def layer_norm(x, g, b):
    xf = x.astype(jnp.float32)
    mu = jnp.mean(xf, axis=-1, keepdims=True)
    var = jnp.mean(jnp.square(xf - mu), axis=-1, keepdims=True)
    return ((xf - mu) * lax.rsqrt(var + LN_EPS) * g + b).astype(x.dtype)


def rms_norm(x, g):
    xf = x.astype(jnp.float32)
    return (xf * lax.rsqrt(jnp.mean(xf * xf, axis=-1, keepdims=True) + RMS_EPS) * g).astype(x.dtype)


def sweep_query_blocks(fn, q_arrays):
    S = q_arrays[0].shape[1]
    nb = S // Q_BLOCK
    blocks = tuple(jnp.moveaxis(a.reshape(a.shape[0], nb, Q_BLOCK, *a.shape[2:]), 1, 0) for a in q_arrays)
    starts = jnp.arange(nb, dtype=jnp.int32) * Q_BLOCK
    out = lax.map(lambda xs: fn(*xs), (blocks, starts))
    out = jnp.moveaxis(out, 0, 1)
    return out.reshape(out.shape[0], S, *out.shape[3:])


def diff_attention(q, k, v, lam_params, subln_g, layer_idx):
    B, S = k.shape[0], k.shape[1]
    lambda_init = 0.8 - 0.6 * math.exp(-0.3 * layer_idx)
    lp = lam_params.astype(jnp.float32)
    lam = jnp.exp(jnp.sum(lp[0] * lp[1])) - jnp.exp(jnp.sum(lp[2] * lp[3])) + lambda_init
    slopes = jnp.asarray(ALIBI_SLOPES, dtype=jnp.float32)
    k_pos = jnp.arange(S, dtype=jnp.int32)
    scale = A_DK ** -0.5

    def block(qs, start):
        (qb,) = qs
        q_pos = start + jnp.arange(Q_BLOCK, dtype=jnp.int32)
        dist = jnp.abs(q_pos[:, None] - k_pos[None, :]).astype(jnp.float32)
        bias = -slopes[:, None, None] * dist
        s = jnp.einsum('bqhmd,bkhmd->bhmqk', qb, k).astype(jnp.float32) * scale + bias[None, :, None]
        p = jax.nn.softmax(s, axis=-1)
        w = p[:, :, 0] - lam * p[:, :, 1]
        return jnp.einsum('bhqk,bkhd->bqhd', w.astype(v.dtype), v)

    o = sweep_query_blocks(block, (q,))
    o = rms_norm(o, subln_g) * (1.0 - lambda_init)
    return o.reshape(B, S, A_HEADS * A_DV)


def rope_tables(S, d):
    inv = ROPE_BASE ** (-jnp.arange(0, d, 2, dtype=jnp.float32) / d)
    ang = jnp.arange(S, dtype=jnp.float32)[:, None] * inv[None, :]
    return jnp.cos(ang), jnp.sin(ang)


def apply_rope(x, cos, sin):
    half = x.shape[-1] // 2
    x1, x2 = x[..., :half], x[..., half:]
    c, s = cos.astype(x.dtype), sin.astype(x.dtype)
    return jnp.concatenate([x1 * c - x2 * s, x1 * s + x2 * c], axis=-1)


def mla_attention(c_q, c_kv, k_r, q_norm_g, w_uq, kv_norm_g, w_ukv):
    B, S, _ = c_q.shape
    q = (rms_norm(c_q, q_norm_g) @ w_uq).reshape(B, S, C_HEADS, C_NOPE + C_ROPE)
    kv = (rms_norm(c_kv, kv_norm_g) @ w_ukv).reshape(B, S, C_HEADS, C_NOPE + C_V)
    q_nope, q_rope = q[..., :C_NOPE], q[..., C_NOPE:]
    k_nope, v = kv[..., :C_NOPE], kv[..., C_NOPE:]
    cos, sin = rope_tables(S, C_ROPE)
    q_rope = apply_rope(q_rope, cos[:, None, :], sin[:, None, :])
    k_rope = apply_rope(k_r, cos, sin)
    scale = (C_NOPE + C_ROPE) ** -0.5

    def block(qs, start):
        qn, qr = qs
        s = (jnp.einsum('bqhd,bkhd->bhqk', qn, k_nope)
             + jnp.einsum('bqhd,bkd->bhqk', qr, k_rope)).astype(jnp.float32) * scale
        p = jax.nn.softmax(s, axis=-1)
        return jnp.einsum('bhqk,bkhd->bqhd', p.astype(v.dtype), v)

    o = sweep_query_blocks(block, (q_nope, q_rope))
    return o.reshape(B, S, C_HEADS * C_V)


def chunked_gated_recurrence(q, k, v, log_f):
    B, S, H, DK = q.shape
    DV = v.shape[-1]
    n = S // CHUNK

    def to_chunks(a):
        return a.astype(jnp.float32).reshape(B, n, CHUNK, H, a.shape[-1]).transpose(1, 0, 3, 2, 4)

    qc, kc, vc, fc = (to_chunks(a) for a in (q, k, v, log_f))
    tri = jnp.tril(jnp.ones((CHUNK, CHUNK), dtype=bool))[:, :, None]

    def step(state, xs):
        qi, ki, vi, fi = xs
        b = jnp.cumsum(fi, axis=2)
        diff = b[:, :, :, None, :] - b[:, :, None, :, :]
        decay = jnp.exp(jnp.where(tri, diff, -jnp.inf))
        scores = jnp.einsum('bhtd,bhsd,bhtsd->bhts', qi, ki, decay)
        o = scores @ vi + jnp.einsum('bhtd,bhdv->bhtv', qi * jnp.exp(b), state)
        b_last = b[:, :, -1:, :]
        new_state = (jnp.exp(b_last[:, :, 0, :])[..., None] * state
                     + jnp.einsum('bhsd,bhsv->bhdv', ki * jnp.exp(b_last - b), vi))
        return new_state, o

    state0 = jnp.zeros((B, H, DK, DV), jnp.float32)
    _, o = lax.scan(step, state0, (qc, kc, vc, fc))
    return o.transpose(1, 0, 3, 2, 4).reshape(B, S, H, DV)


def hgrn2_direction(q, v, f_logits, lb, reverse):
    B, S, H, DK = q.shape
    f = (lb + (1.0 - lb) * jax.nn.sigmoid(f_logits.astype(jnp.float32))).reshape(B, S, H, DK)
    k = 1.0 - f
    log_f = jnp.log(f)
    if reverse:
        q, k, v, log_f = (jnp.flip(a, axis=1) for a in (q, k, v, log_f))
    o = chunked_gated_recurrence(q, k, v, log_f)
    return jnp.flip(o, axis=1) if reverse else o


def hgrn2_mixer(b_q, b_ff, b_fb, b_i, b_g, lb, norm_g):
    B, S, _ = b_q.shape
    q = jax.nn.silu(b_q.astype(jnp.float32)).reshape(B, S, B_HEADS, B_DK)
    v = b_i.astype(jnp.float32).reshape(B, S, B_HEADS, B_DV)
    o = hgrn2_direction(q, v, b_ff, lb[0], False) + hgrn2_direction(q, v, b_fb, lb[1], True)
    o = rms_norm(o, norm_g.reshape(B_HEADS, B_DV)).reshape(B, S, B_HEADS * B_DV)
    return (o * jax.nn.silu(b_g.astype(jnp.float32))).astype(b_q.dtype)


def linear_scan_combine(e1, e2):
    a1, b1 = e1
    a2, b2 = e2
    return a1 * a2, a2 * b1 + b2


def rglru_mixer(d_x, d_g, conv_w, conv_b, w_a, b_a, w_x, b_x, lam):
    B, S, W = d_x.shape
    pad = jnp.pad(d_x, ((0, 0), (CONV_LEFT, CONV_W - 1 - CONV_LEFT), (0, 0)))
    xc = (sum(pad[:, j:j + S] * conv_w[j] for j in range(CONV_W)) + conv_b).astype(jnp.float32)
    xb = xc.reshape(B, S, D_BLOCKS, D_BLOCK_W)
    h_sum = jnp.zeros_like(xc)
    for d, reverse in ((0, False), (1, True)):
        r = jax.nn.sigmoid(jnp.einsum('bsnc,ncd->bsnd', xb, w_a[d]).reshape(B, S, W) + b_a[d])
        i = jax.nn.sigmoid(jnp.einsum('bsnc,ncd->bsnd', xb, w_x[d]).reshape(B, S, W) + b_x[d])
        log_a = -RG_C * r * jax.nn.softplus(-lam[d].astype(jnp.float32))
        a = jnp.exp(log_a)
        u = jnp.sqrt(-jnp.expm1(2.0 * log_a)) * (i * xc)
        _, h = lax.associative_scan(linear_scan_combine, (a, u), axis=1, reverse=reverse)
        h_sum = h_sum + h
    return (h_sum * jax.nn.gelu(d_g.astype(jnp.float32))).astype(d_x.dtype)


def expert_choice_ffn(x, w_router, w_gate, w_up, w_down):
    B, S, D = x.shape
    T = B * S
    cap = CAPACITY_FACTOR * T // N_EXPERTS
    xt = x.reshape(T, D)
    aff = jax.nn.softmax((xt @ w_router).astype(jnp.float32), axis=-1)
    g, idx = lax.top_k(aff.T, cap)
    xe = xt[idx]
    h = jax.nn.silu(jnp.einsum('ecd,edf->ecf', xe, w_gate)) * jnp.einsum('ecd,edf->ecf', xe, w_up)
    ye = jnp.einsum('ecf,efd->ecd', h, w_down) * g[..., None].astype(x.dtype)
    y = jnp.zeros((T, D), ye.dtype).at[idx.reshape(-1)].add(ye.reshape(-1, D))
    return y.reshape(B, S, D).astype(x.dtype)


def token_mixer(x, l, p):
    B, S, _ = x.shape
    u = x @ p['w_in'][l]
    (a_q, a_k, a_v, b_q, b_ff, b_fb, b_i, b_g,
     c_q, c_kv, c_kr, d_x, d_g, gate) = jnp.split(u, IN_SPLITS, axis=-1)
    y_a = diff_attention(a_q.reshape(B, S, A_HEADS, 2, A_DK), a_k.reshape(B, S, A_HEADS, 2, A_DK),
                         a_v.reshape(B, S, A_HEADS, A_DV), p['diff_lambda'][l], p['diff_subln_g'][l], l)
    y_b = hgrn2_mixer(b_q, b_ff, b_fb, b_i, b_g, p['hgrn_lb'][l], p['hgrn_norm_g'][l])
    y_c = mla_attention(c_q, c_kv, c_kr, p['mla_q_norm_g'][l], p['mla_w_uq'][l],
                        p['mla_kv_norm_g'][l], p['mla_w_ukv'][l])
    y_d = rglru_mixer(d_x, d_g, p['rg_conv_w'][l], p['rg_conv_b'][l], p['rg_w_a'][l], p['rg_b_a'][l],
                      p['rg_w_x'][l], p['rg_b_x'][l], p['rg_lambda'][l])
    gate = jax.nn.sigmoid(gate.reshape(B, S, N_BRANCH, D_MODEL))
    mix = sum(gate[:, :, i] * (y @ p['w_branch'][l, i]) for i, y in enumerate((y_a, y_b, y_c, y_d)))
    return mix @ p['w_out'][l]


def encoder_trunk(x, p):
    for l in range(DEPTH):
        x = layer_norm(ALPHA * x + token_mixer(x, l, p), p['ln_g'][l, 0], p['ln_b'][l, 0])
        y = expert_choice_ffn(x, p['w_router'][l], p['w_e_gate'][l], p['w_e_up'][l], p['w_e_down'][l])
        x = layer_norm(ALPHA * x + y, p['ln_g'][l, 1], p['ln_b'][l, 1])
    return x


def setup_inputs(seed: int = 0) -> dict:
    key = jax.random.key(seed)
    ks = jax.random.split(key, 26)
    f32 = jnp.float32
    L = DEPTH

    def nrm(k, shape, scale):
        return jax.random.normal(k, shape, f32) * scale

    def gain(k, shape):
        return 1.0 + nrm(k, shape, 0.02)

    a0 = jax.random.uniform(ks[17], (L, 2, BR_W), f32, 0.9, 0.999)
    p_lam = a0 ** (1.0 / RG_C)
    return {
        'x_prompt': nrm(ks[0], (BATCH, SEQ, D_MODEL), 1.0),
        'x_sample': nrm(ks[1], (DEC_BATCH, DEC_SEQ, D_MODEL), 1.0),
        'w_in': nrm(ks[2], (L, D_MODEL, IN_W), D_MODEL ** -0.5),
        'diff_lambda': nrm(ks[3], (L, 4, A_DK), 0.1),
        'diff_subln_g': gain(ks[4], (L, A_DV)),
        'hgrn_lb_logits': nrm(ks[5], (L, 2, B_HEADS * B_DK), 0.5),
        'hgrn_norm_g': gain(ks[6], (L, B_HEADS * B_DV)),
        'mla_q_norm_g': gain(ks[7], (L, C_Q_LORA)),
        'mla_w_uq': nrm(ks[8], (L, C_Q_LORA, C_HEADS * (C_NOPE + C_ROPE)), C_Q_LORA ** -0.5),
        'mla_kv_norm_g': gain(ks[9], (L, C_KV_LORA)),
        'mla_w_ukv': nrm(ks[10], (L, C_KV_LORA, C_HEADS * (C_NOPE + C_V)), C_KV_LORA ** -0.5),
        'rg_conv_w': nrm(ks[11], (L, CONV_W, BR_W), CONV_W ** -0.5),
        'rg_conv_b': nrm(ks[12], (L, BR_W), 0.02),
        'rg_w_a': nrm(ks[13], (L, 2, D_BLOCKS, D_BLOCK_W, D_BLOCK_W), D_BLOCK_W ** -0.5),
        'rg_b_a': nrm(ks[14], (L, 2, BR_W), 0.02),
        'rg_w_x': nrm(ks[15], (L, 2, D_BLOCKS, D_BLOCK_W, D_BLOCK_W), D_BLOCK_W ** -0.5),
        'rg_b_x': nrm(ks[16], (L, 2, BR_W), 0.02),
        'rg_lambda': jnp.log(p_lam) - jnp.log1p(-p_lam),
        'w_branch': nrm(ks[18], (L, N_BRANCH, BR_W, D_MODEL), BR_W ** -0.5),
        'w_out': nrm(ks[19], (L, D_MODEL, D_MODEL), BETA * D_MODEL ** -0.5),
        'ln_g': gain(ks[20], (L, 2, D_MODEL)),
        'ln_b': nrm(ks[21], (L, 2, D_MODEL), 0.02),
        'w_router': nrm(ks[22], (L, D_MODEL, N_EXPERTS), D_MODEL ** -0.5),
        'w_e_gate': nrm(ks[23], (L, N_EXPERTS, D_MODEL, D_EXPERT), D_MODEL ** -0.5),
        'w_e_up': nrm(ks[24], (L, N_EXPERTS, D_MODEL, D_EXPERT), D_MODEL ** -0.5),
        'w_e_down': nrm(ks[25], (L, N_EXPERTS, D_EXPERT, D_MODEL), BETA * D_EXPERT ** -0.5),
    }


def reference(x_prompt, x_sample, w_in, diff_lambda, diff_subln_g, hgrn_lb_logits, hgrn_norm_g,
              mla_q_norm_g, mla_w_uq, mla_kv_norm_g, mla_w_ukv, rg_conv_w, rg_conv_b, rg_w_a, rg_b_a,
              rg_w_x, rg_b_x, rg_lambda, w_branch, w_out, ln_g, ln_b, w_router, w_e_gate, w_e_up,
              w_e_down):
    lb_all = jnp.cumsum(jax.nn.softmax(hgrn_lb_logits.astype(jnp.float32), axis=0), axis=0)
    lb_all = lb_all - lb_all[:1]
    params = dict(w_in=w_in, diff_lambda=diff_lambda, diff_subln_g=diff_subln_g, hgrn_lb=lb_all,
                  hgrn_norm_g=hgrn_norm_g, mla_q_norm_g=mla_q_norm_g, mla_w_uq=mla_w_uq,
                  mla_kv_norm_g=mla_kv_norm_g, mla_w_ukv=mla_w_ukv, rg_conv_w=rg_conv_w,
                  rg_conv_b=rg_conv_b, rg_w_a=rg_w_a, rg_b_a=rg_b_a, rg_w_x=rg_w_x, rg_b_x=rg_b_x,
                  rg_lambda=rg_lambda, w_branch=w_branch, w_out=w_out, ln_g=ln_g, ln_b=ln_b,
                  w_router=w_router, w_e_gate=w_e_gate, w_e_up=w_e_up, w_e_down=w_e_down)
    y_prompt = encoder_trunk(x_prompt, params)
    y_sample = encoder_trunk(x_sample, params)
    return (y_prompt, y_sample)
```

```python
import functools
import math

import numpy as np
import jax
import jax.numpy as jnp
from jax import lax
from jax.experimental import pallas as pl
from jax.experimental.pallas import tpu as pltpu

F32 = jnp.float32
BF16 = jnp.bfloat16

D_MODEL = 1024
DEPTH = 2
N_BRANCH = 4
BR_W = 256
A_HEADS = 4
A_DK = 32
A_DV = 64
B_HEADS = 4
B_DK = 64
C_HEADS = 4
C_NOPE = 32
C_ROPE = 16
C_QK = C_NOPE + C_ROPE
C_V = 64
C_Q_LORA = 192
C_KV_LORA = 128
C_PAD_W = 384
ROPE_BASE = 10000.0
D_BLOCKS = 4
CONV_W = 4
CONV_LEFT = 2
RG_C = 8.0
N_EXPERTS = 16
D_EXPERT = 2048
CAPACITY_FACTOR = 2
CHUNK = 128
LN_EPS = 1e-5
RMS_EPS = 1e-6
ALPHA = (2 * DEPTH) ** 0.25
ALIBI_SLOPES = tuple(2.0 ** (-8.0 * (h + 1) / A_HEADS) for h in range(A_HEADS))

VMEM_LIMIT = 56 * 1024 * 1024

NT_DIMS = (((1,), (1,)), ((), ()))
TN_DIMS = (((0,), (0,)), ((), ()))


def _cparams(sem):
    return pltpu.CompilerParams(dimension_semantics=sem, vmem_limit_bytes=VMEM_LIMIT)


def _dot(a, b):
    return jnp.dot(a, b, preferred_element_type=F32)


def _dot_nt(a, b):
    return lax.dot_general(a, b, NT_DIMS, preferred_element_type=F32)


def _dot_tn(a, b):
    return lax.dot_general(a, b, TN_DIMS, preferred_element_type=F32)


def _dot_exact(a, b):
    return jnp.dot(a, b, preferred_element_type=F32, precision=lax.Precision.HIGHEST)


def _sigmoid(x):
    return 1.0 / (1.0 + jnp.exp(-x))


def _pick(n, pref):
    t = min(pref, n)
    while n % t:
        t //= 2
    return t


def _in_proj_kernel(x_ref, waq, wakT, wavT, wb, wc, wd, aq_o, akT_o, avT_o, b_o, c_o, d_o):
    xb = x_ref[...].astype(BF16)
    aq_o[...] = (_dot(xb, waq[...]) * (A_DK ** -0.5)).astype(BF16)
    akT_o[0] = _dot_nt(wakT[...], xb).astype(BF16)
    avT_o[0] = _dot_nt(wavT[...], xb).astype(BF16)
    b_o[...] = _dot(xb, wb[...])
    c_o[...] = _dot(xb, wc[...])
    d_o[...] = _dot(xb, wd[...])


def in_proj(x2, B, S, w):
    T = B * S
    tm = _pick(S, 512)
    ns = S // tm
    full = lambda shape: pl.BlockSpec(shape, lambda i: (0,) * len(shape))
    rows = lambda wdt: pl.BlockSpec((tm, wdt), lambda i: (i, 0))
    colsT = pl.BlockSpec((1, BR_W, tm), lambda i: (i // ns, 0, i % ns))
    return pl.pallas_call(
        _in_proj_kernel,
        grid=(T // tm,),
        in_specs=[rows(D_MODEL), full((D_MODEL, BR_W)), full((BR_W, D_MODEL)), full((BR_W, D_MODEL)),
                  full((D_MODEL, 5 * BR_W)), full((D_MODEL, C_PAD_W)), full((D_MODEL, 2 * BR_W))],
        out_specs=[rows(BR_W), colsT, colsT, rows(5 * BR_W), rows(C_PAD_W), rows(2 * BR_W)],
        out_shape=[jax.ShapeDtypeStruct((T, BR_W), BF16),
                   jax.ShapeDtypeStruct((B, BR_W, S), BF16),
                   jax.ShapeDtypeStruct((B, BR_W, S), BF16),
                   jax.ShapeDtypeStruct((T, 5 * BR_W), F32),
                   jax.ShapeDtypeStruct((T, C_PAD_W), F32),
                   jax.ShapeDtypeStruct((T, 2 * BR_W), F32)],
        compiler_params=_cparams(("parallel",)),
        name="in_proj",
    )(x2, w["aq"], w["akT"], w["avT"], w["b"], w["c"], w["d"])


def _online_softmax_step(s, vT_h, m_s, l_s, acc_s, p_idx):
    m_old = m_s[p_idx]
    m_new = jnp.maximum(m_old, jnp.max(s, axis=1, keepdims=True))
    alpha = jnp.exp(m_old - m_new)
    p = jnp.exp(s - m_new)
    l_s[p_idx] = alpha * l_s[p_idx] + jnp.sum(p, axis=1, keepdims=True)
    acc_s[p_idx] = alpha * acc_s[p_idx] + _dot_nt(p.astype(BF16), vT_h)
    m_s[p_idx] = m_new


def _diff_attn_kernel(lam_ref, q_ref, kT_ref, vT_ref, g_ref, o_ref, m_s, l_s, acc_s, *, tq, tk, lambda_init):
    qi = pl.program_id(1)
    ki = pl.program_id(2)

    @pl.when(ki == 0)
    def _():
        m_s[...] = jnp.full(m_s.shape, -jnp.inf, F32)
        l_s[...] = jnp.zeros(l_s.shape, F32)
        acc_s[...] = jnp.zeros(acc_s.shape, F32)

    rel = (qi * tq - ki * tk).astype(F32)
    delta = (lax.broadcasted_iota(jnp.int32, (tq, tk), 0)
             - lax.broadcasted_iota(jnp.int32, (tq, tk), 1)).astype(F32)
    dist = jnp.abs(delta + rel)
    for h in range(A_HEADS):
        bias = dist * (-ALIBI_SLOPES[h])
        vT_h = vT_ref[0, A_DV * h:A_DV * (h + 1), :]
        for m in range(2):
            p_idx = 2 * h + m
            q_hm = q_ref[:, A_DK * p_idx:A_DK * (p_idx + 1)]
            kT_hm = kT_ref[0, A_DK * p_idx:A_DK * (p_idx + 1), :]
            s = _dot(q_hm, kT_hm) + bias
            _online_softmax_step(s, vT_h, m_s, l_s, acc_s, p_idx)

    @pl.when(ki == pl.num_programs(2) - 1)
    def _():
        lam = lam_ref[0]
        for h in range(A_HEADS):
            o = acc_s[2 * h] / l_s[2 * h] - lam * (acc_s[2 * h + 1] / l_s[2 * h + 1])
            ms = jnp.mean(o * o, axis=1, keepdims=True)
            o = o * lax.rsqrt(ms + RMS_EPS) * g_ref[...] * (1.0 - lambda_init)
            o_ref[:, A_DV * h:A_DV * (h + 1)] = o.astype(BF16)


def diff_attention(aq, akT, avT, lam, subln_g, layer_idx, B, S):
    tq = _pick(S, 256)
    tk = _pick(S, 512)
    nq = S // tq
    lambda_init = 0.8 - 0.6 * math.exp(-0.3 * layer_idx)
    kern = functools.partial(_diff_attn_kernel, tq=tq, tk=tk, lambda_init=lambda_init)
    return pl.pallas_call(
        kern,
        grid=(B, nq, S // tk),
        in_specs=[pl.BlockSpec(memory_space=pltpu.SMEM),
                  pl.BlockSpec((tq, BR_W), lambda b, i, j: (b * nq + i, 0)),
                  pl.BlockSpec((1, BR_W, tk), lambda b, i, j: (b, 0, j)),
                  pl.BlockSpec((1, BR_W, tk), lambda b, i, j: (b, 0, j)),
                  pl.BlockSpec((1, A_DV), lambda b, i, j: (0, 0))],
        out_specs=pl.BlockSpec((tq, BR_W), lambda b, i, j: (b * nq + i, 0)),
        out_shape=jax.ShapeDtypeStruct((B * S, BR_W), BF16),
        scratch_shapes=[pltpu.VMEM((2 * A_HEADS, tq, 1), F32),
                        pltpu.VMEM((2 * A_HEADS, tq, 1), F32),
                        pltpu.VMEM((2 * A_HEADS, tq, A_DV), F32)],
        compiler_params=_cparams(("parallel", "parallel", "arbitrary")),
        name="diff_attn",
    )(lam, aq, akT, avT, subln_g)


def _mla_prep_kernel(c_ref, qg_ref, kvg_ref, wuq, wuq_rot, wukT, wuvT, cosq, sinq, cosk, sink, eye_ref,
                     q_o, kT_o, vT_o):
    c = c_ref[...]
    cq = c[:, :C_Q_LORA]
    ckv = c[:, C_Q_LORA:C_Q_LORA + C_KV_LORA]
    ckr = c[:, C_Q_LORA + C_KV_LORA:C_Q_LORA + C_KV_LORA + C_ROPE]
    ckr_rot = c[:, C_Q_LORA + C_KV_LORA + C_ROPE:C_Q_LORA + C_KV_LORA + 2 * C_ROPE]
    nq = (cq * lax.rsqrt(jnp.mean(cq * cq, axis=1, keepdims=True) + RMS_EPS) * qg_ref[...]).astype(BF16)
    q = _dot(nq, wuq[...]) * cosq[...] + _dot(nq, wuq_rot[...]) * sinq[...]
    q_o[...] = (q * (C_QK ** -0.5)).astype(BF16)
    nkv = (ckv * lax.rsqrt(jnp.mean(ckv * ckv, axis=1, keepdims=True) + RMS_EPS) * kvg_ref[...]).astype(BF16)
    knT = _dot_nt(wukT[...], nkv).astype(BF16)
    vT_o[0] = _dot_nt(wuvT[...], nkv).astype(BF16)
    kr = (ckr * cosk[...] + ckr_rot * sink[...]).astype(BF16)
    krT = _dot_nt(eye_ref[...], kr).astype(BF16)
    for h in range(C_HEADS):
        kT_o[0, C_QK * h:C_QK * h + C_NOPE, :] = knT[C_NOPE * h:C_NOPE * (h + 1), :]
        kT_o[0, C_QK * h + C_NOPE:C_QK * (h + 1), :] = krT


def mla_prep(c_all, w, tabs, B, S):
    T = B * S
    tm = _pick(S, 512)
    ns = S // tm
    full = lambda shape: pl.BlockSpec(shape, lambda i: (0,) * len(shape))
    rows = lambda wdt: pl.BlockSpec((tm, wdt), lambda i: (i, 0))
    pos = lambda wdt: pl.BlockSpec((tm, wdt), lambda i: (i % ns, 0))
    qw = C_HEADS * C_QK
    return pl.pallas_call(
        _mla_prep_kernel,
        grid=(T // tm,),
        in_specs=[rows(C_PAD_W), full((1, C_Q_LORA)), full((1, C_KV_LORA)), full((C_Q_LORA, qw)),
                  full((C_Q_LORA, qw)), full((C_HEADS * C_NOPE, C_KV_LORA)), full((C_HEADS * C_V, C_KV_LORA)),
                  pos(qw), pos(qw), pos(C_ROPE), pos(C_ROPE), full((C_ROPE, C_ROPE))],
        out_specs=[rows(qw),
                   pl.BlockSpec((1, qw, tm), lambda i: (i // ns, 0, i % ns)),
                   pl.BlockSpec((1, C_HEADS * C_V, tm), lambda i: (i // ns, 0, i % ns))],
        out_shape=[jax.ShapeDtypeStruct((T, qw), BF16),
                   jax.ShapeDtypeStruct((B, qw, S), BF16),
                   jax.ShapeDtypeStruct((B, C_HEADS * C_V, S), BF16)],
        compiler_params=_cparams(("parallel",)),
        name="mla_prep",
    )(c_all, w["q_norm_g"], w["kv_norm_g"], w["wuq"], w["wuq_rot"], w["wukT"], w["wuvT"],
      tabs["cosq"], tabs["sinq"], tabs["cosk"], tabs["sink"], tabs["eye"])


def _mla_attn_kernel(q_ref, kT_ref, vT_ref, o_ref, m_s, l_s, acc_s):
    ki = pl.program_id(2)

    @pl.when(ki == 0)
    def _():
        m_s[...] = jnp.full(m_s.shape, -jnp.inf, F32)
        l_s[...] = jnp.zeros(l_s.shape, F32)
        acc_s[...] = jnp.zeros(acc_s.shape, F32)

    for h in range(C_HEADS):
        s = _dot(q_ref[:, C_QK * h:C_QK * (h + 1)], kT_ref[0, C_QK * h:C_QK * (h + 1), :])
        _online_softmax_step(s, vT_ref[0, C_V * h:C_V * (h + 1), :], m_s, l_s, acc_s, h)

    @pl.when(ki == pl.num_programs(2) - 1)
    def _():
        for h in range(C_HEADS):
            o_ref[:, C_V * h:C_V * (h + 1)] = (acc_s[h] / l_s[h]).astype(BF16)


def mla_attention(q, kT, vT, B, S):
    tq = _pick(S, 256)
    tk = _pick(S, 512)
    nq = S // tq
    qw = C_HEADS * C_QK
    return pl.pallas_call(
        _mla_attn_kernel,
        grid=(B, nq, S // tk),
        in_specs=[pl.BlockSpec((tq, qw), lambda b, i, j: (b * nq + i, 0)),
                  pl.BlockSpec((1, qw, tk), lambda b, i, j: (b, 0, j)),
                  pl.BlockSpec((1, BR_W, tk), lambda b, i, j: (b, 0, j))],
        out_specs=pl.BlockSpec((tq, BR_W), lambda b, i, j: (b * nq + i, 0)),
        out_shape=jax.ShapeDtypeStruct((B * S, BR_W), BF16),
        scratch_shapes=[pltpu.VMEM((C_HEADS, tq, 1), F32),
                        pltpu.VMEM((C_HEADS, tq, 1), F32),
                        pltpu.VMEM((C_HEADS, tq, C_V), F32)],
        compiler_params=_cparams(("parallel", "parallel", "arbitrary")),
        name="mla_attn",
    )(q, kT, vT)


HGRN_LEVELS = (64, 32, 16, 8, 4, 2, 1)
N_HGRN_SUMS = len(HGRN_LEVELS) + 2


def _hgrn_constants():
    C = CHUNK
    t = np.arange(C)
    sums = np.zeros((2, N_HGRN_SUMS, C, C), np.float32)
    masks = np.zeros((2, len(HGRN_LEVELS) + 1, C, C), np.float32)
    for li, m in enumerate(HGRN_LEVELS):
        blk = t // m
        for tt in range(C):
            if blk[tt] % 2 == 1:
                sums[0, li, tt, blk[tt] * m:tt + 1] = 1.0
            else:
                sums[0, li, tt, tt + 1:(blk[tt] + 1) * m] = 1.0
        masks[0, li] = ((blk[:, None] % 2 == 1) & (blk[None, :] == blk[:, None] - 1)).astype(np.float32)
    masks[0, len(HGRN_LEVELS)] = np.eye(C, dtype=np.float32)
    sums[0, len(HGRN_LEVELS)] = (t[None, :] <= t[:, None]).astype(np.float32)
    sums[0, len(HGRN_LEVELS) + 1] = (t[None, :] > t[:, None]).astype(np.float32)
    sums[1] = sums[0][:, ::-1, ::-1]
    masks[1] = masks[0][:, ::-1, ::-1]
    return sums.reshape(2, N_HGRN_SUMS * C, C), masks


def _hgrn_kernel(qf_ref, ff_ref, vf_ref, qb_ref, fb_ref, vb_ref, lb_ref, sums_ref, mask_ref,
                 of_ref, ob_ref, st_ref):
    C = CHUNK
    nlev = len(HGRN_LEVELS)

    @pl.when(pl.program_id(1) == 0)
    def _():
        st_ref[...] = jnp.zeros(st_ref.shape, F32)

    lane = lax.broadcasted_iota(jnp.int32, (1, 2 * B_DK), 1)
    head_lanes = (lane < B_DK, lane >= B_DK)
    r_i = lax.broadcasted_iota(jnp.int32, (2 * B_DK, 2 * B_DK), 0)
    c_i = lax.broadcasted_iota(jnp.int32, (2 * B_DK, 2 * B_DK), 1)
    same_head = (r_i < B_DK) == (c_i < B_DK)

    for d, (q_ref, f_ref, v_ref, o_ref) in enumerate(((qf_ref, ff_ref, vf_ref, of_ref),
                                                      (qb_ref, fb_ref, vb_ref, ob_ref))):
        x = q_ref[...]
        q = x * _sigmoid(x)
        lb = lb_ref[d]
        f = lb + (1.0 - lb) * _sigmoid(f_ref[...])
        logf = jnp.log(f)
        k = 1.0 - f
        v = v_ref[...]
        hi = logf.astype(BF16)
        lo = (logf - hi.astype(F32)).astype(BF16)
        sums = sums_ref[d]
        E = _dot(sums, hi) + _dot(sums, lo)
        for p in range(B_HEADS // 2):
            sl = slice(2 * B_DK * p, 2 * B_DK * (p + 1))
            Qp, Kp, Vp = q[:, sl], k[:, sl], v[:, sl]
            A = [jnp.zeros((C, C), F32), jnp.zeros((C, C), F32)]
            for l in range(nlev + 1):
                if l < nlev:
                    G = jnp.exp(E[C * l:C * (l + 1), sl])
                    QG = (Qp * G).astype(BF16)
                    KG = Kp * G
                else:
                    QG = Qp.astype(BF16)
                    KG = Kp
                for h in range(2):
                    KGh = jnp.where(head_lanes[h], KG, 0.0).astype(BF16)
                    A[h] = A[h] + mask_ref[d, l] * _dot_nt(QG, KGh)
            o = (_dot(A[0].astype(BF16), jnp.where(head_lanes[0], Vp, 0.0).astype(BF16))
                 + _dot(A[1].astype(BF16), jnp.where(head_lanes[1], Vp, 0.0).astype(BF16)))
            eb = jnp.exp(E[C * nlev:C * (nlev + 1), sl])
            st = st_ref[d, p]
            o = o + _dot_nt((Qp * eb).astype(BF16), st.astype(BF16))
            o_ref[:, sl] = o
            Kd = (Kp * jnp.exp(E[C * (nlev + 1):C * (nlev + 2), sl])).astype(BF16)
            row = C - 1 if d == 0 else 0
            g = eb[row:row + 1, :]
            upd = _dot_tn(Vp.astype(BF16), Kd)
            st_ref[d, p] = st * g + jnp.where(same_head, upd, 0.0)


def hgrn2(b_all, lb, B, S):
    T = B * S
    C = CHUNK
    nc = S // C
    sums, masks = _hgrn_constants()
    sums = jnp.asarray(sums, BF16)
    masks = jnp.asarray(masks, F32)
    fwd = lambda col: pl.BlockSpec((C, BR_W), lambda b, n: (b * nc + n, col))
    bwd = lambda col: pl.BlockSpec((C, BR_W), lambda b, n: (b * nc + nc - 1 - n, col))
    full = lambda shape: pl.BlockSpec(shape, lambda b, n: (0,) * len(shape))
    return pl.pallas_call(
        _hgrn_kernel,
        grid=(B, nc),
        in_specs=[fwd(0), fwd(1), fwd(3), bwd(0), bwd(2), bwd(3),
                  full((2, 1, BR_W)), full(sums.shape), full(masks.shape)],
        out_specs=[pl.BlockSpec((C, BR_W), lambda b, n: (b * nc + n, 0)),
                   pl.BlockSpec((C, BR_W), lambda b, n: (b * nc + nc - 1 - n, 0))],
        out_shape=[jax.ShapeDtypeStruct((T, BR_W), F32), jax.ShapeDtypeStruct((T, BR_W), F32)],
        scratch_shapes=[pltpu.VMEM((2, B_HEADS // 2, 2 * B_DK, 2 * B_DK), F32)],
        compiler_params=_cparams(("parallel", "arbitrary")),
        name="hgrn2",
    )(b_all, b_all, b_all, b_all, b_all, b_all, lb, sums, masks)


HALO = 8


def _rglru_kernel(xf_ref, xfp_ref, xfn_ref, xb_ref, xbp_ref, xbn_ref, cw_ref, cb_ref, wa_ref, ba_ref,
                  wx_ref, bx_ref, sp_ref, hf_ref, hb_ref, carry_ref, *, tc):
    n = pl.program_id(1)
    nc = pl.num_programs(1)

    @pl.when(n == 0)
    def _():
        carry_ref[...] = jnp.zeros(carry_ref.shape, F32)

    row = lax.broadcasted_iota(jnp.int32, (tc, 1), 0)
    for d, (x_ref, xp_ref, xn_ref, o_ref) in enumerate(((xf_ref, xfp_ref, xfn_ref, hf_ref),
                                                        (xb_ref, xbp_ref, xbn_ref, hb_ref))):
        cidx = n if d == 0 else nc - 1 - n
        prev = jnp.where(cidx > 0, xp_ref[...], 0.0)
        nxt = jnp.where(cidx < nc - 1, xn_ref[...], 0.0)
        ext = jnp.concatenate([prev, x_ref[...], nxt], axis=0)
        xc = cb_ref[...]
        for j in range(CONV_W):
            off = HALO - CONV_LEFT + j
            xc = xc + ext[off:off + tc, :] * cw_ref[j:j + 1, :]
        xcb = xc.astype(BF16)
        r = _sigmoid(_dot(xcb, wa_ref[d]) + ba_ref[d])
        i = _sigmoid(_dot(xcb, wx_ref[d]) + bx_ref[d])
        log_a = (-RG_C) * r * sp_ref[d]
        a = jnp.exp(log_a)
        u = jnp.sqrt(1.0 - jnp.exp(2.0 * log_a)) * (i * xc)
        sh = 1
        while sh < tc:
            if d == 0:
                valid = row >= sh
                amt = sh
            else:
                valid = row < tc - sh
                amt = tc - sh
            a_sh = jnp.where(valid, pltpu.roll(a, amt, 0), 1.0)
            u_sh = jnp.where(valid, pltpu.roll(u, amt, 0), 0.0)
            u = a * u_sh + u
            a = a * a_sh
            sh *= 2
        h = u + a * carry_ref[d]
        o_ref[...] = h
        last = tc - 1 if d == 0 else 0
        carry_ref[d] = h[last:last + 1, :]


def rglru(d_all, w, B, S):
    T = B * S
    tc = _pick(S, 512)
    nc = S // tc
    hb = tc // HALO
    nrow8 = T // HALO
    fc = lambda b, n: b * nc + n
    bc = lambda b, n: b * nc + nc - 1 - n
    cur = lambda f: pl.BlockSpec((tc, BR_W), lambda b, n: (f(b, n), 0))
    prv = lambda f: pl.BlockSpec((HALO, BR_W), lambda b, n: (jnp.maximum(f(b, n) * hb - 1, 0), 0))
    nxt = lambda f: pl.BlockSpec((HALO, BR_W), lambda b, n: (jnp.minimum((f(b, n) + 1) * hb, nrow8 - 1), 0))
    full = lambda shape: pl.BlockSpec(shape, lambda b, n: (0,) * len(shape))
    kern = functools.partial(_rglru_kernel, tc=tc)
    return pl.pallas_call(
        kern,
        grid=(B, nc),
        in_specs=[cur(fc), prv(fc), nxt(fc), cur(bc), prv(bc), nxt(bc),
                  full((CONV_W, BR_W)), full((1, BR_W)), full((2, BR_W, BR_W)), full((2, 1, BR_W)),
                  full((2, BR_W, BR_W)), full((2, 1, BR_W)), full((2, 1, BR_W))],
        out_specs=[cur(fc), cur(bc)],
        out_shape=[jax.ShapeDtypeStruct((T, BR_W), F32), jax.ShapeDtypeStruct((T, BR_W), F32)],
        scratch_shapes=[pltpu.VMEM((2, 1, BR_W), F32)],
        compiler_params=_cparams(("parallel", "arbitrary")),
        name="rglru",
    )(d_all, d_all, d_all, d_all, d_all, d_all, w["conv_w"], w["conv_b"], w["wa"], w["ba"], w["wx"],
      w["bx"], w["sp"])


def _layer_norm(z, g, b):
    mu = jnp.mean(z, axis=1, keepdims=True)
    zc = z - mu
    var = jnp.mean(zc * zc, axis=1, keepdims=True)
    return zc * lax.rsqrt(var + LN_EPS) * g + b


def _merge_kernel(x_ref, ya_ref, of_ref, ob_ref, bg_ref, yc_ref, hf_ref, hb_ref, dg_ref, wg_ref, wbr_ref,
                  wo_ref, lng_ref, lnb_ref, hg_ref, havg_ref, wr_ref, x1_ref, aff_ref):
    x = x_ref[...]
    xb = x.astype(BF16)
    o = of_ref[...] + ob_ref[...]
    ms = _dot_exact(o * o, havg_ref[...])
    bg = bg_ref[...]
    yb = o * lax.rsqrt(ms + RMS_EPS) * hg_ref[...] * (bg * _sigmoid(bg))
    dg = dg_ref[...]
    gelu = 0.5 * dg * (1.0 + jnp.tanh(0.7978845608028654 * (dg + 0.044715 * dg * dg * dg)))
    yd = (hf_ref[...] + hb_ref[...]) * gelu
    ys = (ya_ref[...], yb.astype(BF16), yc_ref[...], yd.astype(BF16))
    mix = None
    for i in range(N_BRANCH):
        gate = _sigmoid(_dot(xb, wg_ref[:, D_MODEL * i:D_MODEL * (i + 1)]))
        term = gate * _dot(ys[i], wbr_ref[i])
        mix = term if mix is None else mix + term
    z = ALPHA * x + _dot(mix.astype(BF16), wo_ref[...])
    x1 = _layer_norm(z, lng_ref[...], lnb_ref[...])
    x1_ref[...] = x1
    logits = _dot_exact(x1, wr_ref[...])
    e = jnp.exp(logits - jnp.max(logits, axis=1, keepdims=True))
    aff_ref[...] = e / jnp.sum(e, axis=1, keepdims=True)


def merge(x2, ya, of, ob, b_all, yc, hf, hb, d_all, w):
    T = x2.shape[0]
    tm = _pick(T, 256)
    full = lambda shape: pl.BlockSpec(shape, lambda i: (0,) * len(shape))
    rows = lambda wdt: pl.BlockSpec((tm, wdt), lambda i: (i, 0))
    colblk = lambda c: pl.BlockSpec((tm, BR_W), lambda i: (i, c))
    return pl.pallas_call(
        _merge_kernel,
        grid=(T // tm,),
        in_specs=[rows(D_MODEL), rows(BR_W), rows(BR_W), rows(BR_W), colblk(4), rows(BR_W), rows(BR_W),
                  rows(BR_W), colblk(1), full((D_MODEL, N_BRANCH * D_MODEL)),
                  full((N_BRANCH, BR_W, D_MODEL)), full((D_MODEL, D_MODEL)), full((1, D_MODEL)),
                  full((1, D_MODEL)), full((1, BR_W)), full((BR_W, BR_W)), full((D_MODEL, N_EXPERTS))],
        out_specs=[rows(D_MODEL), rows(N_EXPERTS)],
        out_shape=[jax.ShapeDtypeStruct((T, D_MODEL), F32), jax.ShapeDtypeStruct((T, N_EXPERTS), F32)],
        compiler_params=_cparams(("parallel",)),
        name="merge",
    )(x2, ya, of, ob, b_all, yc, hf, hb, d_all, w["gate"], w["branch"], w["out"], w["ln_g"], w["ln_b"],
      w["hgrn_norm_g"], w["head_avg"], w["router"])


def _select_kernel(aff_ref, sel_ref, *, cap, n_tok):
    aff = aff_ref[...]
    v = lax.bitcast_convert_type(aff, jnp.int32)
    E, R, L = aff.shape

    def count(pred):
        c = jnp.sum(pred.astype(F32), axis=1)
        return jnp.sum(c, axis=1, keepdims=True)

    def value_step(i, thr):
        cand = thr | jnp.left_shift(jnp.int32(1), 30 - i)
        ok = count(v >= cand[:, :, None]) >= cap
        return jnp.where(ok, cand, thr)

    thr = lax.fori_loop(0, 31, value_step, jnp.zeros((E, 1), jnp.int32))
    thr3 = thr[:, :, None]
    gt = v > thr3
    eq = v == thr3
    need = cap - count(gt)
    tok = (lax.broadcasted_iota(jnp.int32, (E, R, L), 1) * L
           + lax.broadcasted_iota(jnp.int32, (E, R, L), 2))
    nbits = max(1, int(n_tok).bit_length())

    def index_step(i, bound):
        cand = bound | jnp.left_shift(jnp.int32(1), nbits - 1 - i)
        ok = count(eq & (tok < cand[:, :, None])) <= need
        return jnp.where(ok, cand, bound)

    bound = lax.fori_loop(0, nbits, index_step, jnp.zeros((E, 1), jnp.int32))
    sel = gt | (eq & (tok < bound[:, :, None]))
    sel_ref[...] = sel.astype(jnp.int32)


def select_tokens(affT3, cap, n_tok):
    E, R, L = affT3.shape
    kern = functools.partial(_select_kernel, cap=cap, n_tok=n_tok)
    return pl.pallas_call(
        kern,
        grid=(1,),
        in_specs=[pl.BlockSpec((E, R, L), lambda i: (0, 0, 0))],
        out_specs=pl.BlockSpec((E, R, L), lambda i: (0, 0, 0)),
        out_shape=jax.ShapeDtypeStruct((E, R, L), jnp.int32),
        compiler_params=_cparams(("arbitrary",)),
        name="select_tokens",
    )(affT3)


FFN_FCHUNK = 512


def _ffn_kernel(idx_hbm, x_hbm, g_ref, wg_ref, wu_ref, wd_ref, ye_ref, idx_smem, xbuf, idx_sem, row_sem,
                *, tm, n_blocks):
    blk = pl.program_id(0) * pl.num_programs(1) + pl.program_id(1)

    def idx_copy(b, slot):
        return pltpu.make_async_copy(idx_hbm.at[b], idx_smem.at[slot], idx_sem)

    def row_copy(src_row, slot, j):
        return pltpu.make_async_copy(x_hbm.at[pl.ds(src_row, 1), :], xbuf.at[slot, pl.ds(j, 1), :],
                                     row_sem.at[slot])

    def start_gather(b, slot):
        cp = idx_copy(b, slot)
        cp.start()
        cp.wait()

        def issue(j, carry):
            row_copy(idx_smem[slot, j], slot, j).start()
            return carry

        lax.fori_loop(0, tm, issue, 0)

    @pl.when(blk == 0)
    def _():
        start_gather(0, 0)

    slot = blk % 2

    @pl.when(blk + 1 < n_blocks)
    def _():
        start_gather(blk + 1, 1 - slot)

    def drain(j, carry):
        row_copy(0, slot, j).wait()
        return carry

    lax.fori_loop(0, tm, drain, 0)

    xb = xbuf[slot].astype(BF16)
    acc = jnp.zeros((tm, D_MODEL), F32)
    for c in range(D_EXPERT // FFN_FCHUNK):
        fs = slice(FFN_FCHUNK * c, FFN_FCHUNK * (c + 1))
        hg = _dot(xb, wg_ref[0, :, fs])
        hu = _dot(xb, wu_ref[0, :, fs])
        h = (hg * _sigmoid(hg) * hu).astype(BF16)
        acc = acc + _dot(h, wd_ref[0, fs, :])
    ye_ref[...] = acc * g_ref[...]


def expert_ffn(idx, x1, g, wg, wu, wd):
    E, cap = idx.shape
    tm = _pick(cap, 512)
    R = cap // tm
    kern = functools.partial(_ffn_kernel, tm=tm, n_blocks=E * R)
    return pl.pallas_call(
        kern,
        grid=(E, R),
        in_specs=[pl.BlockSpec(memory_space=pl.ANY),
                  pl.BlockSpec(memory_space=pl.ANY),
                  pl.BlockSpec((tm, 1), lambda e, r: (e * R + r, 0)),
                  pl.BlockSpec((1, D_MODEL, D_EXPERT), lambda e, r: (e, 0, 0)),
                  pl.BlockSpec((1, D_MODEL, D_EXPERT), lambda e, r: (e, 0, 0)),
                  pl.BlockSpec((1, D_EXPERT, D_MODEL), lambda e, r: (e, 0, 0))],
        out_specs=pl.BlockSpec((tm, D_MODEL), lambda e, r: (e * R + r, 0)),
        out_shape=jax.ShapeDtypeStruct((E * cap, D_MODEL), F32),
        scratch_shapes=[pltpu.SMEM((2, tm), jnp.int32),
                        pltpu.VMEM((2, tm, D_MODEL), F32),
                        pltpu.SemaphoreType.DMA(()),
                        pltpu.SemaphoreType.DMA((2,))],
        compiler_params=_cparams(("arbitrary", "arbitrary")),
        name="expert_ffn",
    )(idx.reshape(E * R, tm), x1, g.reshape(E * cap, 1), wg, wu, wd)


def _ln2_kernel(x_ref, y_ref, g_ref, b_ref, o_ref):
    o_ref[...] = _layer_norm(ALPHA * x_ref[...] + y_ref[...], g_ref[...], b_ref[...])


def post_ffn_norm(x1, y, g, b):
    T = x1.shape[0]
    tm = _pick(T, 512)
    rows = pl.BlockSpec((tm, D_MODEL), lambda i: (i, 0))
    vec = pl.BlockSpec((1, D_MODEL), lambda i: (0, 0))
    return pl.pallas_call(
        _ln2_kernel,
        grid=(T // tm,),
        in_specs=[rows, rows, vec, vec],
        out_specs=rows,
        out_shape=jax.ShapeDtypeStruct((T, D_MODEL), F32),
        compiler_params=_cparams(("parallel",)),
        name="post_ffn_norm",
    )(x1, y, g, b)


def _rot_cols(w, start):
    half = C_ROPE // 2
    return jnp.concatenate([-w[:, start + half:start + C_ROPE], w[:, start:start + half]], axis=1)


def _block_diag(w):
    n, c, _ = w.shape
    out = jnp.zeros((n * c, n * c), w.dtype)
    for i in range(n):
        out = out.at[i * c:(i + 1) * c, i * c:(i + 1) * c].set(w[i])
    return out


def _prep_layer(l, p, lb_all):
    w_in = p["w_in"][l]
    o = 0
    aq = w_in[:, o:o + BR_W]; o += BR_W
    ak = w_in[:, o:o + BR_W]; o += BR_W
    av = w_in[:, o:o + BR_W]; o += BR_W
    wb = w_in[:, o:o + 5 * BR_W]; o += 5 * BR_W
    c0 = o
    wcq = w_in[:, o:o + C_Q_LORA]; o += C_Q_LORA
    wckv = w_in[:, o:o + C_KV_LORA]; o += C_KV_LORA
    wckr = w_in[:, o:o + C_ROPE]; o += C_ROPE
    wd = w_in[:, o:o + 2 * BR_W]; o += 2 * BR_W
    wgate = w_in[:, o:o + N_BRANCH * D_MODEL]
    used = C_Q_LORA + C_KV_LORA + 2 * C_ROPE
    wc = jnp.concatenate([wcq, wckv, wckr, _rot_cols(w_in, c0 + C_Q_LORA + C_KV_LORA),
                          jnp.zeros((D_MODEL, C_PAD_W - used), F32)], axis=1)
    proj = dict(aq=aq.astype(BF16), akT=ak.T.astype(BF16), avT=av.T.astype(BF16), b=wb.astype(BF16),
                c=wc.astype(BF16), d=wd.astype(BF16))

    wuq = p["mla_w_uq"][l]
    wuq_rot = jnp.zeros_like(wuq)
    for h in range(C_HEADS):
        r0 = C_QK * h + C_NOPE
        wuq_rot = wuq_rot.at[:, r0:r0 + C_ROPE].set(_rot_cols(wuq, r0))
    wukv = p["mla_w_ukv"][l].reshape(C_KV_LORA, C_HEADS, C_NOPE + C_V)
    mla = dict(q_norm_g=p["mla_q_norm_g"][l][None, :], kv_norm_g=p["mla_kv_norm_g"][l][None, :],
               wuq=wuq.astype(BF16), wuq_rot=wuq_rot.astype(BF16),
               wukT=wukv[:, :, :C_NOPE].reshape(C_KV_LORA, C_HEADS * C_NOPE).T.astype(BF16),
               wuvT=wukv[:, :, C_NOPE:].reshape(C_KV_LORA, C_HEADS * C_V).T.astype(BF16))

    rg = dict(conv_w=p["rg_conv_w"][l], conv_b=p["rg_conv_b"][l][None, :],
              wa=jnp.stack([_block_diag(p["rg_w_a"][l, d]) for d in range(2)]).astype(BF16),
              ba=p["rg_b_a"][l][:, None, :],
              wx=jnp.stack([_block_diag(p["rg_w_x"][l, d]) for d in range(2)]).astype(BF16),
              bx=p["rg_b_x"][l][:, None, :],
              sp=jax.nn.softplus(-p["rg_lambda"][l].astype(F32))[:, None, :])

    head = np.arange(BR_W) // B_DK
    mrg = dict(gate=wgate.astype(BF16), branch=p["w_branch"][l].astype(BF16), out=p["w_out"][l].astype(BF16),
               ln_g=p["ln_g"][l, 0][None, :], ln_b=p["ln_b"][l, 0][None, :],
               hgrn_norm_g=p["hgrn_norm_g"][l][None, :],
               head_avg=jnp.asarray((head[:, None] == head[None, :]).astype(np.float32) / B_DK),
               router=p["w_router"][l])

    lp = p["diff_lambda"][l].astype(F32)
    lambda_init = 0.8 - 0.6 * math.exp(-0.3 * l)
    lam = (jnp.exp(jnp.sum(lp[0] * lp[1])) - jnp.exp(jnp.sum(lp[2] * lp[3])) + lambda_init).reshape(1)
    return dict(proj=proj, mla=mla, rg=rg, mrg=mrg, lam=lam, subln_g=p["diff_subln_g"][l][None, :],
                lb=lb_all[l][:, None, :],
                ffn=(p["w_e_gate"][l].astype(BF16), p["w_e_up"][l].astype(BF16), p["w_e_down"][l].astype(BF16)),
                ln2_g=p["ln_g"][l, 1][None, :], ln2_b=p["ln_b"][l, 1][None, :])


def _rope_tables(S):
    half = C_ROPE // 2
    inv = ROPE_BASE ** (-jnp.arange(0, C_ROPE, 2, dtype=F32) / C_ROPE)
    ang = jnp.arange(S, dtype=F32)[:, None] * inv[None, :]
    cos, sin = jnp.cos(ang), jnp.sin(ang)
    cosk = jnp.concatenate([cos, cos], axis=1)
    sink = jnp.concatenate([sin, sin], axis=1)
    ones = jnp.ones((S, C_NOPE), F32)
    zeros = jnp.zeros((S, C_NOPE), F32)
    cosq = jnp.concatenate([jnp.concatenate([ones, cosk], axis=1)] * C_HEADS, axis=1)
    sinq = jnp.concatenate([jnp.concatenate([zeros, sink], axis=1)] * C_HEADS, axis=1)
    return dict(cosq=cosq, sinq=sinq, cosk=cosk, sink=sink, eye=jnp.eye(C_ROPE, dtype=BF16))


def _layer(x2, B, S, l, w, tabs):
    T = B * S
    aq, akT, avT, b_all, c_all, d_all = in_proj(x2, B, S, w["proj"])
    ya = diff_attention(aq, akT, avT, w["lam"], w["subln_g"], l, B, S)
    of, ob = hgrn2(b_all, w["lb"], B, S)
    cq, ckT, cvT = mla_prep(c_all, w["mla"], tabs, B, S)
    yc = mla_attention(cq, ckT, cvT, B, S)
    hf, hb = rglru(d_all, w["rg"], B, S)
    x1, aff = merge(x2, ya, of, ob, b_all, yc, hf, hb, d_all, w["mrg"])

    cap = CAPACITY_FACTOR * T // N_EXPERTS
    affT = aff.T
    sel = select_tokens(affT.reshape(N_EXPERTS, T // 128, 128), cap, T).reshape(N_EXPERTS, T)
    order = jnp.argsort(1 - sel, axis=1, stable=True)[:, :cap].astype(jnp.int32)
    g = jnp.take_along_axis(affT, order, axis=1)
    ye = expert_ffn(order, x1, g, *w["ffn"])
    y = jnp.zeros((T, D_MODEL), F32).at[order.reshape(-1)].add(ye)
    return post_ffn_norm(x1, y, w["ln2_g"], w["ln2_b"])


def _trunk(x, weights):
    B, S, _ = x.shape
    tabs = _rope_tables(S)
    x2 = x.reshape(B * S, D_MODEL)
    for l in range(DEPTH):
        x2 = _layer(x2, B, S, l, weights[l], tabs)
    return x2.reshape(B, S, D_MODEL)


def kernel(x_prompt, x_sample, w_in, diff_lambda, diff_subln_g, hgrn_lb_logits, hgrn_norm_g, mla_q_norm_g,
           mla_w_uq, mla_kv_norm_g, mla_w_ukv, rg_conv_w, rg_conv_b, rg_w_a, rg_b_a, rg_w_x, rg_b_x, rg_lambda,
           w_branch, w_out, ln_g, ln_b, w_router, w_e_gate, w_e_up, w_e_down):
    lb_all = jnp.cumsum(jax.nn.softmax(hgrn_lb_logits.astype(F32), axis=0), axis=0)
    lb_all = lb_all - lb_all[:1]
    p = dict(w_in=w_in, diff_lambda=diff_lambda, diff_subln_g=diff_subln_g, hgrn_norm_g=hgrn_norm_g,
             mla_q_norm_g=mla_q_norm_g, mla_w_uq=mla_w_uq, mla_kv_norm_g=mla_kv_norm_g, mla_w_ukv=mla_w_ukv,
             rg_conv_w=rg_conv_w, rg_conv_b=rg_conv_b, rg_w_a=rg_w_a, rg_b_a=rg_b_a, rg_w_x=rg_w_x,
             rg_b_x=rg_b_x, rg_lambda=rg_lambda, w_branch=w_branch, w_out=w_out, ln_g=ln_g, ln_b=ln_b,
             w_router=w_router, w_e_gate=w_e_gate, w_e_up=w_e_up, w_e_down=w_e_down)
    weights = [_prep_layer(l, p, lb_all) for l in range(DEPTH)]
    return (_trunk(x_prompt, weights), _trunk(x_sample, weights))
```

```python
import functools
import math

import numpy as np
import jax
import jax.numpy as jnp
from jax import lax
from jax.experimental import pallas as pl
from jax.experimental.pallas import tpu as pltpu

F32 = jnp.float32
BF16 = jnp.bfloat16

D_MODEL = 1024
DEPTH = 2
N_BRANCH = 4
BR_W = 256
A_HEADS = 4
A_DK = 32
A_DV = 64
B_HEADS = 4
B_DK = 64
C_HEADS = 4
C_NOPE = 32
C_ROPE = 16
C_QK = C_NOPE + C_ROPE
C_V = 64
C_Q_LORA = 192
C_KV_LORA = 128
C_PAD_W = 384
ROPE_BASE = 10000.0
D_BLOCKS = 4
CONV_W = 4
CONV_LEFT = 2
RG_C = 8.0
N_EXPERTS = 16
D_EXPERT = 2048
CAPACITY_FACTOR = 2
CHUNK = 128
LN_EPS = 1e-5
RMS_EPS = 1e-6
ALPHA = (2 * DEPTH) ** 0.25
ALIBI_SLOPES = tuple(2.0 ** (-8.0 * (h + 1) / A_HEADS) for h in range(A_HEADS))
A_NP = 2 * A_HEADS
A_QKW = 48
V_AUG_W = 128
ATT_TQ = 256
ATT_TK = 1024

VMEM_LIMIT = 56 * 1024 * 1024

NT_DIMS = (((1,), (1,)), ((), ()))
TN_DIMS = (((0,), (0,)), ((), ()))


def _cparams(sem):
    return pltpu.CompilerParams(dimension_semantics=sem, vmem_limit_bytes=VMEM_LIMIT)


def _dot(a, b):
    return jnp.dot(a, b, preferred_element_type=F32)


def _dot_nt(a, b):
    return lax.dot_general(a, b, NT_DIMS, preferred_element_type=F32)


def _dot_tn(a, b):
    return lax.dot_general(a, b, TN_DIMS, preferred_element_type=F32)


def _dot_exact(a, b):
    return jnp.dot(a, b, preferred_element_type=F32, precision=lax.Precision.HIGHEST)


def _sigmoid(x):
    return 1.0 / (1.0 + jnp.exp(-x))


def _pick(n, pref):
    t = min(pref, n)
    while n % t:
        t //= 2
    return t


def _in_proj_kernel(x_ref, waq, wakT, wav, qpos, kpos, vones, wb, wc, wd, aq_o, akT_o, av_o, b_o, c_o, d_o):
    xb = x_ref[...].astype(BF16)
    aq_o[...] = (_dot(xb, waq[...]) * (A_DK ** -0.5) + qpos[...]).astype(BF16)
    akT_o[0] = (_dot_nt(wakT[...], xb) + kpos[...]).astype(BF16)
    av_o[0] = (_dot(xb, wav[...]) + vones[...]).astype(BF16)
    b_o[...] = _dot(xb, wb[...])
    c_o[...] = _dot(xb, wc[...])
    d_o[...] = _dot(xb, wd[...])


def in_proj(x2, B, S, w, tabs):
    T = B * S
    tm = _pick(S, 512)
    ns = S // tm
    aw = A_NP * A_QKW
    vw = A_HEADS * V_AUG_W
    full = lambda shape: pl.BlockSpec(shape, lambda i: (0,) * len(shape))
    rows = lambda wdt: pl.BlockSpec((tm, wdt), lambda i: (i, 0))
    return pl.pallas_call(
        _in_proj_kernel,
        grid=(T // tm,),
        in_specs=[rows(D_MODEL), full((D_MODEL, aw)), full((aw, D_MODEL)), full((D_MODEL, vw)),
                  pl.BlockSpec((tm, aw), lambda i: (i % ns, 0)),
                  pl.BlockSpec((aw, tm), lambda i: (0, i % ns)),
                  full((1, vw)),
                  full((D_MODEL, 5 * BR_W)), full((D_MODEL, C_PAD_W)), full((D_MODEL, 2 * BR_W))],
        out_specs=[rows(aw),
                   pl.BlockSpec((1, aw, tm), lambda i: (i // ns, 0, i % ns)),
                   pl.BlockSpec((1, tm, vw), lambda i: (i // ns, i % ns, 0)),
                   rows(5 * BR_W), rows(C_PAD_W), rows(2 * BR_W)],
        out_shape=[jax.ShapeDtypeStruct((T, aw), BF16),
                   jax.ShapeDtypeStruct((B, aw, S), BF16),
                   jax.ShapeDtypeStruct((B, S, vw), BF16),
                   jax.ShapeDtypeStruct((T, 5 * BR_W), F32),
                   jax.ShapeDtypeStruct((T, C_PAD_W), F32),
                   jax.ShapeDtypeStruct((T, 2 * BR_W), F32)],
        compiler_params=_cparams(("parallel",)),
        name="in_proj",
    )(x2, w["aq"], w["akT"], w["av"], tabs["a_qpos"], tabs["a_kpos"], tabs["v_ones"], w["b"], w["c"], w["d"])


def _softmax_tile(n_prob, qk, v_of, c_of, corr_of, m_s, acc_s):
    s_next = qk(0)
    for p in range(n_prob):
        s = s_next
        if p + 1 < n_prob:
            s_next = qk(p + 1)
        corr = corr_of(p)
        if corr is not None:
            s = s + corr
        c = c_of(p)
        m_old = m_s[p]
        m_new = jnp.maximum(m_old, jnp.max(s, axis=1, keepdims=True) + c)
        alpha = jnp.exp(m_old - m_new)
        pr = jnp.exp(s - (m_new - c))
        acc_s[p] = alpha * acc_s[p] + _dot(pr.astype(BF16), v_of(p))
        m_s[p] = m_new


def _diff_attn_kernel(lam_ref, q_ref, kT_ref, v_ref, g_ref, o_ref, qs, m_s, acc_s, *, tq, tk, nk, lambda_init):
    qi = pl.program_id(1)
    lane = lax.broadcasted_iota(jnp.int32, (1, A_NP * A_QKW), 1)
    is_pos = (lane % A_QKW) >= A_DK
    qa = q_ref[...].astype(F32)
    qneg = jnp.where(is_pos, -qa, qa)
    for p in range(A_NP):
        qs[p] = qa[:, A_QKW * p:A_QKW * (p + 1)].astype(BF16)
        qs[A_NP + p] = qneg[:, A_QKW * p:A_QKW * (p + 1)].astype(BF16)
    m_s[...] = jnp.full(m_s.shape, -jnp.inf, F32)
    acc_s[...] = jnp.zeros(acc_s.shape, F32)

    def tile(j, variant, straddle):
        j0 = pl.multiple_of(j * tk, tk)
        off = (qi * tq - j0).astype(F32)
        if straddle:
            d = (lax.broadcasted_iota(jnp.int32, (tq, tk), 0)
                 - lax.broadcasted_iota(jnp.int32, (tq, tk), 1)).astype(F32) + off
            dpos = jnp.maximum(d, 0.0)

        def qk(p):
            return _dot(qs[variant * A_NP + p], kT_ref[0, A_QKW * p:A_QKW * (p + 1), pl.ds(j0, tk)])

        _softmax_tile(
            A_NP, qk,
            v_of=lambda p: v_ref[0, pl.ds(j0, tk), V_AUG_W * (p // 2):V_AUG_W * (p // 2 + 1)],
            c_of=lambda p: (off if variant == 0 else -off) * ALIBI_SLOPES[p // 2],
            corr_of=lambda p: dpos * (-2.0 * ALIBI_SLOPES[p // 2]) if straddle else None,
            m_s=m_s, acc_s=acc_s)

    def left(j, carry):
        tile(j, 1, False)
        return carry

    def right(j, carry):
        tile(j, 0, False)
        return carry

    jd = qi // (tk // tq)
    lax.fori_loop(0, jd, left, 0)
    tile(jd, 0, True)
    lax.fori_loop(jd + 1, nk, right, 0)

    lam = lam_ref[0]
    for h in range(A_HEADS):
        a0 = acc_s[2 * h]
        a1 = acc_s[2 * h + 1]
        o = a0[:, :A_DV] / a0[:, A_DV:A_DV + 1] - lam * (a1[:, :A_DV] / a1[:, A_DV:A_DV + 1])
        ms = jnp.mean(o * o, axis=1, keepdims=True)
        o = o * lax.rsqrt(ms + RMS_EPS) * g_ref[...] * (1.0 - lambda_init)
        o_ref[:, A_DV * h:A_DV * (h + 1)] = o.astype(BF16)


def _att_tiles(S):
    tk = _pick(S, ATT_TK)
    tq = _pick(tk, ATT_TQ)
    return tq, tk


def diff_attention(aq, akT, av, lam, subln_g, layer_idx, B, S):
    tq, tk = _att_tiles(S)
    nq = S // tq
    aw = A_NP * A_QKW
    vw = A_HEADS * V_AUG_W
    lambda_init = 0.8 - 0.6 * math.exp(-0.3 * layer_idx)
    kern = functools.partial(_diff_attn_kernel, tq=tq, tk=tk, nk=S // tk, lambda_init=lambda_init)
    return pl.pallas_call(
        kern,
        grid=(B, nq),
        in_specs=[pl.BlockSpec(memory_space=pltpu.SMEM),
                  pl.BlockSpec((tq, aw), lambda b, i: (b * nq + i, 0)),
                  pl.BlockSpec((1, aw, S), lambda b, i: (b, 0, 0)),
                  pl.BlockSpec((1, S, vw), lambda b, i: (b, 0, 0)),
                  pl.BlockSpec((1, A_DV), lambda b, i: (0, 0))],
        out_specs=pl.BlockSpec((tq, BR_W), lambda b, i: (b * nq + i, 0)),
        out_shape=jax.ShapeDtypeStruct((B * S, BR_W), BF16),
        scratch_shapes=[pltpu.VMEM((2 * A_NP, tq, A_QKW), BF16),
                        pltpu.VMEM((A_NP, tq, 1), F32),
                        pltpu.VMEM((A_NP, tq, V_AUG_W), F32)],
        compiler_params=_cparams(("parallel", "arbitrary")),
        name="diff_attn",
    )(lam, aq, akT, av, subln_g)


def _mla_prep_kernel(c_ref, qg_ref, kvg_ref, wuq, wuq_rot, wukT, wuv, vones, cosq, sinq, cosk, sink, eye_ref,
                     q_o, kT_o, v_o):
    c = c_ref[...]
    cq = c[:, :C_Q_LORA]
    ckv = c[:, C_Q_LORA:C_Q_LORA + C_KV_LORA]
    ckr = c[:, C_Q_LORA + C_KV_LORA:C_Q_LORA + C_KV_LORA + C_ROPE]
    ckr_rot = c[:, C_Q_LORA + C_KV_LORA + C_ROPE:C_Q_LORA + C_KV_LORA + 2 * C_ROPE]
    nq = (cq * lax.rsqrt(jnp.mean(cq * cq, axis=1, keepdims=True) + RMS_EPS) * qg_ref[...]).astype(BF16)
    q = _dot(nq, wuq[...]) * cosq[...] + _dot(nq, wuq_rot[...]) * sinq[...]
    q_o[...] = (q * (C_QK ** -0.5)).astype(BF16)
    nkv = (ckv * lax.rsqrt(jnp.mean(ckv * ckv, axis=1, keepdims=True) + RMS_EPS) * kvg_ref[...]).astype(BF16)
    knT = _dot_nt(wukT[...], nkv).astype(BF16)
    v_o[0] = (_dot(nkv, wuv[...]) + vones[...]).astype(BF16)
    kr = (ckr * cosk[...] + ckr_rot * sink[...]).astype(BF16)
    krT = _dot_nt(eye_ref[...], kr).astype(BF16)
    for h in range(C_HEADS):
        kT_o[0, C_QK * h:C_QK * h + C_NOPE, :] = knT[C_NOPE * h:C_NOPE * (h + 1), :]
        kT_o[0, C_QK * h + C_NOPE:C_QK * (h + 1), :] = krT


def mla_prep(c_all, w, tabs, B, S):
    T = B * S
    tm = _pick(S, 512)
    ns = S // tm
    full = lambda shape: pl.BlockSpec(shape, lambda i: (0,) * len(shape))
    rows = lambda wdt: pl.BlockSpec((tm, wdt), lambda i: (i, 0))
    pos = lambda wdt: pl.BlockSpec((tm, wdt), lambda i: (i % ns, 0))
    qw = C_HEADS * C_QK
    vw = C_HEADS * V_AUG_W
    return pl.pallas_call(
        _mla_prep_kernel,
        grid=(T // tm,),
        in_specs=[rows(C_PAD_W), full((1, C_Q_LORA)), full((1, C_KV_LORA)), full((C_Q_LORA, qw)),
                  full((C_Q_LORA, qw)), full((C_HEADS * C_NOPE, C_KV_LORA)), full((C_KV_LORA, vw)),
                  full((1, vw)), pos(qw), pos(qw), pos(C_ROPE), pos(C_ROPE), full((C_ROPE, C_ROPE))],
        out_specs=[rows(qw),
                   pl.BlockSpec((1, qw, tm), lambda i: (i // ns, 0, i % ns)),
                   pl.BlockSpec((1, tm, vw), lambda i: (i // ns, i % ns, 0))],
        out_shape=[jax.ShapeDtypeStruct((T, qw), BF16),
                   jax.ShapeDtypeStruct((B, qw, S), BF16),
                   jax.ShapeDtypeStruct((B, S, vw), BF16)],
        compiler_params=_cparams(("parallel",)),
        name="mla_prep",
    )(c_all, w["q_norm_g"], w["kv_norm_g"], w["wuq"], w["wuq_rot"], w["wukT"], w["wuv"], tabs["v_ones"],
      tabs["cosq"], tabs["sinq"], tabs["cosk"], tabs["sink"], tabs["eye"])


def _mla_attn_kernel(q_ref, kT_ref, v_ref, o_ref, qs, m_s, acc_s, *, tk, nk):
    for h in range(C_HEADS):
        qs[h] = q_ref[:, C_QK * h:C_QK * (h + 1)]
    m_s[...] = jnp.full(m_s.shape, -jnp.inf, F32)
    acc_s[...] = jnp.zeros(acc_s.shape, F32)

    def tile(j, carry):
        j0 = pl.multiple_of(j * tk, tk)
        _softmax_tile(
            C_HEADS,
            qk=lambda h: _dot(qs[h], kT_ref[0, C_QK * h:C_QK * (h + 1), pl.ds(j0, tk)]),
            v_of=lambda h: v_ref[0, pl.ds(j0, tk), V_AUG_W * h:V_AUG_W * (h + 1)],
            c_of=lambda h: 0.0,
            corr_of=lambda h: None,
            m_s=m_s, acc_s=acc_s)
        return carry

    lax.fori_loop(0, nk, tile, 0)
    for h in range(C_HEADS):
        a = acc_s[h]
        o_ref[:, C_V * h:C_V * (h + 1)] = (a[:, :C_V] / a[:, C_V:C_V + 1]).astype(BF16)


def mla_attention(q, kT, v, B, S):
    tq, tk = _att_tiles(S)
    nq = S // tq
    qw = C_HEADS * C_QK
    vw = C_HEADS * V_AUG_W
    kern = functools.partial(_mla_attn_kernel, tk=tk, nk=S // tk)
    return pl.pallas_call(
        kern,
        grid=(B, nq),
        in_specs=[pl.BlockSpec((tq, qw), lambda b, i: (b * nq + i, 0)),
                  pl.BlockSpec((1, qw, S), lambda b, i: (b, 0, 0)),
                  pl.BlockSpec((1, S, vw), lambda b, i: (b, 0, 0))],
        out_specs=pl.BlockSpec((tq, BR_W), lambda b, i: (b * nq + i, 0)),
        out_shape=jax.ShapeDtypeStruct((B * S, BR_W), BF16),
        scratch_shapes=[pltpu.VMEM((C_HEADS, tq, C_QK), BF16),
                        pltpu.VMEM((C_HEADS, tq, 1), F32),
                        pltpu.VMEM((C_HEADS, tq, V_AUG_W), F32)],
        compiler_params=_cparams(("parallel", "arbitrary")),
        name="mla_attn",
    )(q, kT, v)


HGRN_LEVELS = (64, 32, 16, 8, 4, 2, 1)
N_HGRN_SUMS = len(HGRN_LEVELS) + 2


def _hgrn_constants():
    C = CHUNK
    t = np.arange(C)
    sums = np.zeros((2, N_HGRN_SUMS, C, C), np.float32)
    masks = np.zeros((2, len(HGRN_LEVELS) + 1, C, C), np.float32)
    for li, m in enumerate(HGRN_LEVELS):
        blk = t // m
        for tt in range(C):
            if blk[tt] % 2 == 1:
                sums[0, li, tt, blk[tt] * m:tt + 1] = 1.0
            else:
                sums[0, li, tt, tt + 1:(blk[tt] + 1) * m] = 1.0
        masks[0, li] = ((blk[:, None] % 2 == 1) & (blk[None, :] == blk[:, None] - 1)).astype(np.float32)
    masks[0, len(HGRN_LEVELS)] = np.eye(C, dtype=np.float32)
    sums[0, len(HGRN_LEVELS)] = (t[None, :] <= t[:, None]).astype(np.float32)
    sums[0, len(HGRN_LEVELS) + 1] = (t[None, :] > t[:, None]).astype(np.float32)
    sums[1] = sums[0][:, ::-1, ::-1]
    masks[1] = masks[0][:, ::-1, ::-1]
    return sums.reshape(2, N_HGRN_SUMS * C, C), masks


def _hgrn_kernel(qf_ref, ff_ref, vf_ref, qb_ref, fb_ref, vb_ref, lb_ref, sums_ref, mask_ref,
                 of_ref, ob_ref, st_ref):
    C = CHUNK
    nlev = len(HGRN_LEVELS)

    @pl.when(pl.program_id(1) == 0)
    def _():
        st_ref[...] = jnp.zeros(st_ref.shape, F32)

    lane = lax.broadcasted_iota(jnp.int32, (1, 2 * B_DK), 1)
    head_lanes = (lane < B_DK, lane >= B_DK)
    r_i = lax.broadcasted_iota(jnp.int32, (2 * B_DK, 2 * B_DK), 0)
    c_i = lax.broadcasted_iota(jnp.int32, (2 * B_DK, 2 * B_DK), 1)
    same_head = (r_i < B_DK) == (c_i < B_DK)

    for d, (q_ref, f_ref, v_ref, o_ref) in enumerate(((qf_ref, ff_ref, vf_ref, of_ref),
                                                      (qb_ref, fb_ref, vb_ref, ob_ref))):
        x = q_ref[...]
        q = x * _sigmoid(x)
        lb = lb_ref[d]
        f = lb + (1.0 - lb) * _sigmoid(f_ref[...])
        logf = jnp.log(f)
        k = 1.0 - f
        v = v_ref[...]
        hi = logf.astype(BF16)
        lo = (logf - hi.astype(F32)).astype(BF16)
        sums = sums_ref[d]
        E = _dot(sums, hi) + _dot(sums, lo)
        for p in range(B_HEADS // 2):
            sl = slice(2 * B_DK * p, 2 * B_DK * (p + 1))
            Qp, Kp, Vp = q[:, sl], k[:, sl], v[:, sl]
            A = [jnp.zeros((C, C), F32), jnp.zeros((C, C), F32)]
            for l in range(nlev + 1):
                if l < nlev:
                    G = jnp.exp(E[C * l:C * (l + 1), sl])
                    QG = (Qp * G).astype(BF16)
                    KG = Kp * G
                else:
                    QG = Qp.astype(BF16)
                    KG = Kp
                for h in range(2):
                    KGh = jnp.where(head_lanes[h], KG, 0.0).astype(BF16)
                    A[h] = A[h] + mask_ref[d, l] * _dot_nt(QG, KGh)
            o = (_dot(A[0].astype(BF16), jnp.where(head_lanes[0], Vp, 0.0).astype(BF16))
                 + _dot(A[1].astype(BF16), jnp.where(head_lanes[1], Vp, 0.0).astype(BF16)))
            eb = jnp.exp(E[C * nlev:C * (nlev + 1), sl])
            st = st_ref[d, p]
            o = o + _dot_nt((Qp * eb).astype(BF16), st.astype(BF16))
            o_ref[:, sl] = o
            Kd = (Kp * jnp.exp(E[C * (nlev + 1):C * (nlev + 2), sl])).astype(BF16)
            row = C - 1 if d == 0 else 0
            g = eb[row:row + 1, :]
            upd = _dot_tn(Vp.astype(BF16), Kd)
            st_ref[d, p] = st * g + jnp.where(same_head, upd, 0.0)


def hgrn2(b_all, lb, B, S):
    T = B * S
    C = CHUNK
    nc = S // C
    sums, masks = _hgrn_constants()
    sums = jnp.asarray(sums, BF16)
    masks = jnp.asarray(masks, F32)
    fwd = lambda col: pl.BlockSpec((C, BR_W), lambda b, n: (b * nc + n, col))
    bwd = lambda col: pl.BlockSpec((C, BR_W), lambda b, n: (b * nc + nc - 1 - n, col))
    full = lambda shape: pl.BlockSpec(shape, lambda b, n: (0,) * len(shape))
    return pl.pallas_call(
        _hgrn_kernel,
        grid=(B, nc),
        in_specs=[fwd(0), fwd(1), fwd(3), bwd(0), bwd(2), bwd(3),
                  full((2, 1, BR_W)), full(sums.shape), full(masks.shape)],
        out_specs=[pl.BlockSpec((C, BR_W), lambda b, n: (b * nc + n, 0)),
                   pl.BlockSpec((C, BR_W), lambda b, n: (b * nc + nc - 1 - n, 0))],
        out_shape=[jax.ShapeDtypeStruct((T, BR_W), F32), jax.ShapeDtypeStruct((T, BR_W), F32)],
        scratch_shapes=[pltpu.VMEM((2, B_HEADS // 2, 2 * B_DK, 2 * B_DK), F32)],
        compiler_params=_cparams(("parallel", "arbitrary")),
        name="hgrn2",
    )(b_all, b_all, b_all, b_all, b_all, b_all, lb, sums, masks)


HALO = 8


def _rglru_kernel(xf_ref, xfp_ref, xfn_ref, xb_ref, xbp_ref, xbn_ref, cw_ref, cb_ref, wa_ref, ba_ref,
                  wx_ref, bx_ref, sp_ref, hf_ref, hb_ref, carry_ref, *, tc):
    n = pl.program_id(1)
    nc = pl.num_programs(1)

    @pl.when(n == 0)
    def _():
        carry_ref[...] = jnp.zeros(carry_ref.shape, F32)

    row = lax.broadcasted_iota(jnp.int32, (tc, 1), 0)
    for d, (x_ref, xp_ref, xn_ref, o_ref) in enumerate(((xf_ref, xfp_ref, xfn_ref, hf_ref),
                                                        (xb_ref, xbp_ref, xbn_ref, hb_ref))):
        cidx = n if d == 0 else nc - 1 - n
        prev = jnp.where(cidx > 0, xp_ref[...], 0.0)
        nxt = jnp.where(cidx < nc - 1, xn_ref[...], 0.0)
        ext = jnp.concatenate([prev, x_ref[...], nxt], axis=0)
        xc = cb_ref[...]
        for j in range(CONV_W):
            off = HALO - CONV_LEFT + j
            xc = xc + ext[off:off + tc, :] * cw_ref[j:j + 1, :]
        xcb = xc.astype(BF16)
        r = _sigmoid(_dot(xcb, wa_ref[d]) + ba_ref[d])
        i = _sigmoid(_dot(xcb, wx_ref[d]) + bx_ref[d])
        log_a = (-RG_C) * r * sp_ref[d]
        a = jnp.exp(log_a)
        u = jnp.sqrt(1.0 - jnp.exp(2.0 * log_a)) * (i * xc)
        sh = 1
        while sh < tc:
            if d == 0:
                valid = row >= sh
                amt = sh
            else:
                valid = row < tc - sh
                amt = tc - sh
            a_sh = jnp.where(valid, pltpu.roll(a, amt, 0), 1.0)
            u_sh = jnp.where(valid, pltpu.roll(u, amt, 0), 0.0)
            u = a * u_sh + u
            a = a * a_sh
            sh *= 2
        h = u + a * carry_ref[d]
        o_ref[...] = h
        last = tc - 1 if d == 0 else 0
        carry_ref[d] = h[last:last + 1, :]


def rglru(d_all, w, B, S):
    T = B * S
    tc = _pick(S, 512)
    nc = S // tc
    hb = tc // HALO
    nrow8 = T // HALO
    fc = lambda b, n: b * nc + n
    bc = lambda b, n: b * nc + nc - 1 - n
    cur = lambda f: pl.BlockSpec((tc, BR_W), lambda b, n: (f(b, n), 0))
    prv = lambda f: pl.BlockSpec((HALO, BR_W), lambda b, n: (jnp.maximum(f(b, n) * hb - 1, 0), 0))
    nxt = lambda f: pl.BlockSpec((HALO, BR_W), lambda b, n: (jnp.minimum((f(b, n) + 1) * hb, nrow8 - 1), 0))
    full = lambda shape: pl.BlockSpec(shape, lambda b, n: (0,) * len(shape))
    kern = functools.partial(_rglru_kernel, tc=tc)
    return pl.pallas_call(
        kern,
        grid=(B, nc),
        in_specs=[cur(fc), prv(fc), nxt(fc), cur(bc), prv(bc), nxt(bc),
                  full((CONV_W, BR_W)), full((1, BR_W)), full((2, BR_W, BR_W)), full((2, 1, BR_W)),
                  full((2, BR_W, BR_W)), full((2, 1, BR_W)), full((2, 1, BR_W))],
        out_specs=[cur(fc), cur(bc)],
        out_shape=[jax.ShapeDtypeStruct((T, BR_W), F32), jax.ShapeDtypeStruct((T, BR_W), F32)],
        scratch_shapes=[pltpu.VMEM((2, 1, BR_W), F32)],
        compiler_params=_cparams(("parallel", "arbitrary")),
        name="rglru",
    )(d_all, d_all, d_all, d_all, d_all, d_all, w["conv_w"], w["conv_b"], w["wa"], w["ba"], w["wx"],
      w["bx"], w["sp"])


def _layer_norm(z, g, b):
    mu = jnp.mean(z, axis=1, keepdims=True)
    zc = z - mu
    var = jnp.mean(zc * zc, axis=1, keepdims=True)
    return zc * lax.rsqrt(var + LN_EPS) * g + b


def _merge_kernel(x_ref, ya_ref, of_ref, ob_ref, bg_ref, yc_ref, hf_ref, hb_ref, dg_ref, wg_ref, wbr_ref,
                  wo_ref, lng_ref, lnb_ref, hg_ref, havg_ref, wr_ref, x1_ref, aff_ref):
    x = x_ref[...]
    xb = x.astype(BF16)
    o = of_ref[...] + ob_ref[...]
    ms = _dot_exact(o * o, havg_ref[...])
    bg = bg_ref[...]
    yb = o * lax.rsqrt(ms + RMS_EPS) * hg_ref[...] * (bg * _sigmoid(bg))
    dg = dg_ref[...]
    gelu = 0.5 * dg * (1.0 + jnp.tanh(0.7978845608028654 * (dg + 0.044715 * dg * dg * dg)))
    yd = (hf_ref[...] + hb_ref[...]) * gelu
    ys = (ya_ref[...], yb.astype(BF16), yc_ref[...], yd.astype(BF16))
    mix = None
    for i in range(N_BRANCH):
        gate = _sigmoid(_dot(xb, wg_ref[:, D_MODEL * i:D_MODEL * (i + 1)]))
        term = gate * _dot(ys[i], wbr_ref[i])
        mix = term if mix is None else mix + term
    z = ALPHA * x + _dot(mix.astype(BF16), wo_ref[...])
    x1 = _layer_norm(z, lng_ref[...], lnb_ref[...])
    x1_ref[...] = x1
    logits = _dot_exact(x1, wr_ref[...])
    e = jnp.exp(logits - jnp.max(logits, axis=1, keepdims=True))
    aff_ref[...] = e / jnp.sum(e, axis=1, keepdims=True)


def merge(x2, ya, of, ob, b_all, yc, hf, hb, d_all, w):
    T = x2.shape[0]
    tm = _pick(T, 256)
    full = lambda shape: pl.BlockSpec(shape, lambda i: (0,) * len(shape))
    rows = lambda wdt: pl.BlockSpec((tm, wdt), lambda i: (i, 0))
    colblk = lambda c: pl.BlockSpec((tm, BR_W), lambda i: (i, c))
    return pl.pallas_call(
        _merge_kernel,
        grid=(T // tm,),
        in_specs=[rows(D_MODEL), rows(BR_W), rows(BR_W), rows(BR_W), colblk(4), rows(BR_W), rows(BR_W),
                  rows(BR_W), colblk(1), full((D_MODEL, N_BRANCH * D_MODEL)),
                  full((N_BRANCH, BR_W, D_MODEL)), full((D_MODEL, D_MODEL)), full((1, D_MODEL)),
                  full((1, D_MODEL)), full((1, BR_W)), full((BR_W, BR_W)), full((D_MODEL, N_EXPERTS))],
        out_specs=[rows(D_MODEL), rows(N_EXPERTS)],
        out_shape=[jax.ShapeDtypeStruct((T, D_MODEL), F32), jax.ShapeDtypeStruct((T, N_EXPERTS), F32)],
        compiler_params=_cparams(("parallel",)),
        name="merge",
    )(x2, ya, of, ob, b_all, yc, hf, hb, d_all, w["gate"], w["branch"], w["out"], w["ln_g"], w["ln_b"],
      w["hgrn_norm_g"], w["head_avg"], w["router"])


def _select_kernel(aff_ref, sel_ref, *, cap, n_tok):
    aff = aff_ref[...]
    v = lax.bitcast_convert_type(aff, jnp.int32)
    E, R, L = aff.shape

    def count(pred):
        c = jnp.sum(pred.astype(F32), axis=1)
        return jnp.sum(c, axis=1, keepdims=True)

    def value_step(i, thr):
        cand = thr | jnp.left_shift(jnp.int32(1), 30 - i)
        ok = count(v >= cand[:, :, None]) >= cap
        return jnp.where(ok, cand, thr)

    thr = lax.fori_loop(0, 31, value_step, jnp.zeros((E, 1), jnp.int32))
    thr3 = thr[:, :, None]
    gt = v > thr3
    eq = v == thr3
    need = cap - count(gt)
    tok = (lax.broadcasted_iota(jnp.int32, (E, R, L), 1) * L
           + lax.broadcasted_iota(jnp.int32, (E, R, L), 2))
    nbits = max(1, int(n_tok).bit_length())

    def index_step(i, bound):
        cand = bound | jnp.left_shift(jnp.int32(1), nbits - 1 - i)
        ok = count(eq & (tok < cand[:, :, None])) <= need
        return jnp.where(ok, cand, bound)

    bound = lax.fori_loop(0, nbits, index_step, jnp.zeros((E, 1), jnp.int32))
    sel = gt | (eq & (tok < bound[:, :, None]))
    sel_ref[...] = sel.astype(jnp.int32)


def select_tokens(affT3, cap, n_tok):
    E, R, L = affT3.shape
    kern = functools.partial(_select_kernel, cap=cap, n_tok=n_tok)
    return pl.pallas_call(
        kern,
        grid=(1,),
        in_specs=[pl.BlockSpec((E, R, L), lambda i: (0, 0, 0))],
        out_specs=pl.BlockSpec((E, R, L), lambda i: (0, 0, 0)),
        out_shape=jax.ShapeDtypeStruct((E, R, L), jnp.int32),
        compiler_params=_cparams(("arbitrary",)),
        name="select_tokens",
    )(affT3)


FFN_FCHUNK = 512


def _ffn_kernel(idx_hbm, x_hbm, g_ref, wg_ref, wu_ref, wd_ref, ye_ref, idx_smem, xbuf, idx_sem, row_sem,
                *, tm, n_blocks):
    blk = pl.program_id(0) * pl.num_programs(1) + pl.program_id(1)

    last = n_blocks - 1
    slot = blk % 2
    nslot = 1 - slot

    def idx_copy(b, s):
        return pltpu.make_async_copy(idx_hbm.at[b], idx_smem.at[s], idx_sem.at[s])

    def row_copy(src_row, s, j):
        return pltpu.make_async_copy(x_hbm.at[pl.ds(src_row, 1), :], xbuf.at[s, pl.ds(j, 1), :],
                                     row_sem.at[s])

    def wait_rows(s):
        pltpu.make_async_copy(x_hbm.at[pl.ds(0, tm), :], xbuf.at[s], row_sem.at[s]).wait()

    @pl.when(blk == 0)
    def _():
        first = idx_copy(0, 0)
        first.start()
        first.wait()

        def issue(j, carry):
            row_copy(idx_smem[0, j], 0, j).start()
            return carry

        lax.fori_loop(0, tm, issue, 0)
        idx_copy(jnp.minimum(1, last), 1).start()

    idx_copy(0, nslot).wait()
    wait_rows(slot)
    xb = xbuf[slot].astype(BF16)
    acc = jnp.zeros((tm, D_MODEL), F32)
    n_chunks = D_EXPERT // FFN_FCHUNK
    per_chunk = tm // n_chunks
    for c in range(n_chunks):
        for j in range(per_chunk * c, per_chunk * (c + 1)):
            row_copy(idx_smem[nslot, j], nslot, j).start()
        fs = slice(FFN_FCHUNK * c, FFN_FCHUNK * (c + 1))
        hg = _dot(xb, wg_ref[0, :, fs])
        hu = _dot(xb, wu_ref[0, :, fs])
        h = (hg * _sigmoid(hg) * hu).astype(BF16)
        acc = acc + _dot(h, wd_ref[0, fs, :])
    ye_ref[...] = acc * g_ref[...]
    idx_copy(jnp.minimum(blk + 2, last), slot).start()

    @pl.when(blk == last)
    def _():
        wait_rows(nslot)
        idx_copy(0, slot).wait()


def expert_ffn(idx, x1, g, wg, wu, wd):
    E, cap = idx.shape
    tm = _pick(cap, 512)
    R = cap // tm
    kern = functools.partial(_ffn_kernel, tm=tm, n_blocks=E * R)
    return pl.pallas_call(
        kern,
        grid=(E, R),
        in_specs=[pl.BlockSpec(memory_space=pl.ANY),
                  pl.BlockSpec(memory_space=pl.ANY),
                  pl.BlockSpec((tm, 1), lambda e, r: (e * R + r, 0)),
                  pl.BlockSpec((1, D_MODEL, D_EXPERT), lambda e, r: (e, 0, 0)),
                  pl.BlockSpec((1, D_MODEL, D_EXPERT), lambda e, r: (e, 0, 0)),
                  pl.BlockSpec((1, D_EXPERT, D_MODEL), lambda e, r: (e, 0, 0))],
        out_specs=pl.BlockSpec((tm, D_MODEL), lambda e, r: (e * R + r, 0)),
        out_shape=jax.ShapeDtypeStruct((E * cap, D_MODEL), F32),
        scratch_shapes=[pltpu.SMEM((2, tm), jnp.int32),
                        pltpu.VMEM((2, tm, D_MODEL), F32),
                        pltpu.SemaphoreType.DMA((2,)),
                        pltpu.SemaphoreType.DMA((2,))],
        compiler_params=pltpu.CompilerParams(dimension_semantics=("arbitrary", "arbitrary"),
                                             vmem_limit_bytes=VMEM_LIMIT, disable_bounds_checks=True),
        name="expert_ffn",
    )(idx.reshape(E * R, tm), x1, g.reshape(E * cap, 1), wg, wu, wd)


def _ln2_kernel(x_ref, y_ref, g_ref, b_ref, o_ref):
    o_ref[...] = _layer_norm(ALPHA * x_ref[...] + y_ref[...], g_ref[...], b_ref[...])


def post_ffn_norm(x1, y, g, b):
    T = x1.shape[0]
    tm = _pick(T, 512)
    rows = pl.BlockSpec((tm, D_MODEL), lambda i: (i, 0))
    vec = pl.BlockSpec((1, D_MODEL), lambda i: (0, 0))
    return pl.pallas_call(
        _ln2_kernel,
        grid=(T // tm,),
        in_specs=[rows, rows, vec, vec],
        out_specs=rows,
        out_shape=jax.ShapeDtypeStruct((T, D_MODEL), F32),
        compiler_params=_cparams(("parallel",)),
        name="post_ffn_norm",
    )(x1, y, g, b)


def _rot_cols(w, start):
    half = C_ROPE // 2
    return jnp.concatenate([-w[:, start + half:start + C_ROPE], w[:, start:start + half]], axis=1)


def _block_diag(w):
    n, c, _ = w.shape
    out = jnp.zeros((n * c, n * c), w.dtype)
    for i in range(n):
        out = out.at[i * c:(i + 1) * c, i * c:(i + 1) * c].set(w[i])
    return out


def _prep_layer(l, p, lb_all):
    w_in = p["w_in"][l]
    o = 0
    aq = w_in[:, o:o + BR_W]; o += BR_W
    ak = w_in[:, o:o + BR_W]; o += BR_W
    av = w_in[:, o:o + BR_W]; o += BR_W
    wb = w_in[:, o:o + 5 * BR_W]; o += 5 * BR_W
    c0 = o
    wcq = w_in[:, o:o + C_Q_LORA]; o += C_Q_LORA
    wckv = w_in[:, o:o + C_KV_LORA]; o += C_KV_LORA
    wckr = w_in[:, o:o + C_ROPE]; o += C_ROPE
    wd = w_in[:, o:o + 2 * BR_W]; o += 2 * BR_W
    wgate = w_in[:, o:o + N_BRANCH * D_MODEL]
    used = C_Q_LORA + C_KV_LORA + 2 * C_ROPE
    wc = jnp.concatenate([wcq, wckv, wckr, _rot_cols(w_in, c0 + C_Q_LORA + C_KV_LORA),
                          jnp.zeros((D_MODEL, C_PAD_W - used), F32)], axis=1)
    padq = jnp.zeros((D_MODEL, A_QKW - A_DK), F32)
    padv = jnp.zeros((D_MODEL, V_AUG_W - A_DV), F32)
    aq_w = jnp.concatenate([t for p_ in range(A_NP) for t in (aq[:, A_DK * p_:A_DK * (p_ + 1)], padq)], axis=1)
    ak_w = jnp.concatenate([t for p_ in range(A_NP) for t in (ak[:, A_DK * p_:A_DK * (p_ + 1)], padq)], axis=1)
    av_w = jnp.concatenate([t for h in range(A_HEADS) for t in (av[:, A_DV * h:A_DV * (h + 1)], padv)], axis=1)
    proj = dict(aq=aq_w.astype(BF16), akT=ak_w.T.astype(BF16), av=av_w.astype(BF16), b=wb.astype(BF16),
                c=wc.astype(BF16), d=wd.astype(BF16))

    wuq = p["mla_w_uq"][l]
    wuq_rot = jnp.zeros_like(wuq)
    for h in range(C_HEADS):
        r0 = C_QK * h + C_NOPE
        wuq_rot = wuq_rot.at[:, r0:r0 + C_ROPE].set(_rot_cols(wuq, r0))
    wukv = p["mla_w_ukv"][l].reshape(C_KV_LORA, C_HEADS, C_NOPE + C_V)
    wuv = jnp.concatenate([wukv[:, :, C_NOPE:], jnp.zeros((C_KV_LORA, C_HEADS, V_AUG_W - C_V), F32)], axis=2)
    mla = dict(q_norm_g=p["mla_q_norm_g"][l][None, :], kv_norm_g=p["mla_kv_norm_g"][l][None, :],
               wuq=wuq.astype(BF16), wuq_rot=wuq_rot.astype(BF16),
               wukT=wukv[:, :, :C_NOPE].reshape(C_KV_LORA, C_HEADS * C_NOPE).T.astype(BF16),
               wuv=wuv.reshape(C_KV_LORA, C_HEADS * V_AUG_W).astype(BF16))

    rg = dict(conv_w=p["rg_conv_w"][l], conv_b=p["rg_conv_b"][l][None, :],
              wa=jnp.stack([_block_diag(p["rg_w_a"][l, d]) for d in range(2)]).astype(BF16),
              ba=p["rg_b_a"][l][:, None, :],
              wx=jnp.stack([_block_diag(p["rg_w_x"][l, d]) for d in range(2)]).astype(BF16),
              bx=p["rg_b_x"][l][:, None, :],
              sp=jax.nn.softplus(-p["rg_lambda"][l].astype(F32))[:, None, :])

    head = np.arange(BR_W) // B_DK
    mrg = dict(gate=wgate.astype(BF16), branch=p["w_branch"][l].astype(BF16), out=p["w_out"][l].astype(BF16),
               ln_g=p["ln_g"][l, 0][None, :], ln_b=p["ln_b"][l, 0][None, :],
               hgrn_norm_g=p["hgrn_norm_g"][l][None, :],
               head_avg=jnp.asarray((head[:, None] == head[None, :]).astype(np.float32) / B_DK),
               router=p["w_router"][l])

    lp = p["diff_lambda"][l].astype(F32)
    lambda_init = 0.8 - 0.6 * math.exp(-0.3 * l)
    lam = (jnp.exp(jnp.sum(lp[0] * lp[1])) - jnp.exp(jnp.sum(lp[2] * lp[3])) + lambda_init).reshape(1)
    return dict(proj=proj, mla=mla, rg=rg, mrg=mrg, lam=lam, subln_g=p["diff_subln_g"][l][None, :],
                lb=lb_all[l][:, None, :],
                ffn=(p["w_e_gate"][l].astype(BF16), p["w_e_up"][l].astype(BF16), p["w_e_down"][l].astype(BF16)),
                ln2_g=p["ln_g"][l, 1][None, :], ln2_b=p["ln_b"][l, 1][None, :])


def _rope_tables(S):
    half = C_ROPE // 2
    inv = ROPE_BASE ** (-jnp.arange(0, C_ROPE, 2, dtype=F32) / C_ROPE)
    ang = jnp.arange(S, dtype=F32)[:, None] * inv[None, :]
    cos, sin = jnp.cos(ang), jnp.sin(ang)
    cosk = jnp.concatenate([cos, cos], axis=1)
    sink = jnp.concatenate([sin, sin], axis=1)
    ones = jnp.ones((S, C_NOPE), F32)
    zeros = jnp.zeros((S, C_NOPE), F32)
    cosq = jnp.concatenate([jnp.concatenate([ones, cosk], axis=1)] * C_HEADS, axis=1)
    sinq = jnp.concatenate([jnp.concatenate([zeros, sink], axis=1)] * C_HEADS, axis=1)
    tq, tk = _att_tiles(S)
    pos = jnp.arange(S, dtype=jnp.int32)
    i_rel = (pos % tq).astype(F32)
    j_rel = pos % tk
    j_hi = (4 * (j_rel // 4)).astype(F32)
    j_lo = (j_rel % 4).astype(F32)
    one = jnp.ones((S,), F32)
    qcols = jnp.stack([i_rel, one, one], axis=1)
    qpad = jnp.zeros((S, A_QKW - A_DK - 3), F32)
    qblock = jnp.concatenate([jnp.zeros((S, A_DK), F32), qcols, qpad], axis=1)
    a_qpos = jnp.concatenate([qblock] * A_NP, axis=1)
    kblocks = []
    for p_ in range(A_NP):
        sl = ALIBI_SLOPES[p_ // 2]
        krows = jnp.stack([sl * one, -sl * j_hi, -sl * j_lo], axis=0)
        kblocks += [jnp.zeros((A_DK, S), F32), krows, jnp.zeros((A_QKW - A_DK - 3, S), F32)]
    a_kpos = jnp.concatenate(kblocks, axis=0)
    v_ones = jnp.zeros((1, A_HEADS * V_AUG_W), F32).at[0, A_DV::V_AUG_W].set(1.0)
    return dict(cosq=cosq, sinq=sinq, cosk=cosk, sink=sink, eye=jnp.eye(C_ROPE, dtype=BF16),
                a_qpos=a_qpos, a_kpos=a_kpos, v_ones=v_ones)


def _layer(x2, B, S, l, w, tabs):
    T = B * S
    aq, akT, av, b_all, c_all, d_all = in_proj(x2, B, S, w["proj"], tabs)
    ya = diff_attention(aq, akT, av, w["lam"], w["subln_g"], l, B, S)
    of, ob = hgrn2(b_all, w["lb"], B, S)
    cq, ckT, cv = mla_prep(c_all, w["mla"], tabs, B, S)
    yc = mla_attention(cq, ckT, cv, B, S)
    hf, hb = rglru(d_all, w["rg"], B, S)
    x1, aff = merge(x2, ya, of, ob, b_all, yc, hf, hb, d_all, w["mrg"])

    cap = CAPACITY_FACTOR * T // N_EXPERTS
    affT = aff.T
    sel = select_tokens(affT.reshape(N_EXPERTS, T // 128, 128), cap, T).reshape(N_EXPERTS, T)
    order = jnp.argsort(1 - sel, axis=1, stable=True)[:, :cap].astype(jnp.int32)
    g = jnp.take_along_axis(affT, order, axis=1)
    ye = expert_ffn(order, x1, g, *w["ffn"])
    y = jnp.zeros((T, D_MODEL), F32).at[order.reshape(-1)].add(ye)
    return post_ffn_norm(x1, y, w["ln2_g"], w["ln2_b"])


def _trunk(x, weights):
    B, S, _ = x.shape
    tabs = _rope_tables(S)
    x2 = x.reshape(B * S, D_MODEL)
    for l in range(DEPTH):
        x2 = _layer(x2, B, S, l, weights[l], tabs)
    return x2.reshape(B, S, D_MODEL)


def kernel(x_prompt, x_sample, w_in, diff_lambda, diff_subln_g, hgrn_lb_logits, hgrn_norm_g, mla_q_norm_g,
           mla_w_uq, mla_kv_norm_g, mla_w_ukv, rg_conv_w, rg_conv_b, rg_w_a, rg_b_a, rg_w_x, rg_b_x, rg_lambda,
           w_branch, w_out, ln_g, ln_b, w_router, w_e_gate, w_e_up, w_e_down):
    lb_all = jnp.cumsum(jax.nn.softmax(hgrn_lb_logits.astype(F32), axis=0), axis=0)
    lb_all = lb_all - lb_all[:1]
    p = dict(w_in=w_in, diff_lambda=diff_lambda, diff_subln_g=diff_subln_g, hgrn_norm_g=hgrn_norm_g,
             mla_q_norm_g=mla_q_norm_g, mla_w_uq=mla_w_uq, mla_kv_norm_g=mla_kv_norm_g, mla_w_ukv=mla_w_ukv,
             rg_conv_w=rg_conv_w, rg_conv_b=rg_conv_b, rg_w_a=rg_w_a, rg_b_a=rg_b_a, rg_w_x=rg_w_x,
             rg_b_x=rg_b_x, rg_lambda=rg_lambda, w_branch=w_branch, w_out=w_out, ln_g=ln_g, ln_b=ln_b,
             w_router=w_router, w_e_gate=w_e_gate, w_e_up=w_e_up, w_e_down=w_e_down)
    weights = [_prep_layer(l, p, lb_all) for l in range(DEPTH)]
    return (_trunk(x_prompt, weights), _trunk(x_sample, weights))
```

```python
import functools
import math

import numpy as np
import jax
import jax.numpy as jnp
from jax import lax
from jax.experimental import pallas as pl
from jax.experimental.pallas import tpu as pltpu

F32 = jnp.float32
BF16 = jnp.bfloat16

D_MODEL = 1024
DEPTH = 2
N_BRANCH = 4
BR_W = 256
A_HEADS = 4
A_DK = 32
A_DV = 64
B_HEADS = 4
B_DK = 64
C_HEADS = 4
C_NOPE = 32
C_ROPE = 16
C_QK = C_NOPE + C_ROPE
C_V = 64
C_Q_LORA = 192
C_KV_LORA = 128
C_PAD_W = 384
ROPE_BASE = 10000.0
D_BLOCKS = 4
CONV_W = 4
CONV_LEFT = 2
RG_C = 8.0
N_EXPERTS = 16
D_EXPERT = 2048
CAPACITY_FACTOR = 2
CHUNK = 128
LN_EPS = 1e-5
RMS_EPS = 1e-6
ALPHA = (2 * DEPTH) ** 0.25
ALIBI_SLOPES = tuple(2.0 ** (-8.0 * (h + 1) / A_HEADS) for h in range(A_HEADS))
A_NP = 2 * A_HEADS
A_QKW = 48
V_AUG_W = 128
ATT_TQ = 256
ATT_TK = 1024

VMEM_LIMIT = 56 * 1024 * 1024

NT_DIMS = (((1,), (1,)), ((), ()))
TN_DIMS = (((0,), (0,)), ((), ()))


def _cparams(sem):
    return pltpu.CompilerParams(dimension_semantics=sem, vmem_limit_bytes=VMEM_LIMIT)


def _dot(a, b):
    return jnp.dot(a, b, preferred_element_type=F32)


def _dot_nt(a, b):
    return lax.dot_general(a, b, NT_DIMS, preferred_element_type=F32)


def _dot_tn(a, b):
    return lax.dot_general(a, b, TN_DIMS, preferred_element_type=F32)


def _dot_split(a, b_bf16):
    hi = a.astype(BF16)
    lo = (a - hi.astype(F32)).astype(BF16)
    return _dot(hi, b_bf16) + _dot(lo, b_bf16)


def _sigmoid(x):
    return 1.0 / (1.0 + jnp.exp(-x))


def _pick(n, pref):
    t = min(pref, n)
    while n % t:
        t //= 2
    return t


def _in_proj_kernel(x_ref, waq, wakT, wav, qpos, kpos, vones, wb, wc, wd, aq_o, akT_o, av_o, b_o, c_o, d_o):
    xb = x_ref[...].astype(BF16)
    aq_o[...] = (_dot(xb, waq[...]) * (A_DK ** -0.5) + qpos[...]).astype(BF16)
    akT_o[0] = (_dot_nt(wakT[...], xb) + kpos[...]).astype(BF16)
    av_o[0] = (_dot(xb, wav[...]) + vones[...]).astype(BF16)
    b_o[...] = _dot(xb, wb[...])
    c_o[...] = _dot(xb, wc[...])
    d_o[...] = _dot(xb, wd[...])


def in_proj(x2, B, S, w, tabs):
    T = B * S
    tm = _pick(S, 512)
    ns = S // tm
    aw = A_NP * A_QKW
    vw = A_HEADS * V_AUG_W
    full = lambda shape: pl.BlockSpec(shape, lambda i: (0,) * len(shape))
    rows = lambda wdt: pl.BlockSpec((tm, wdt), lambda i: (i, 0))
    return pl.pallas_call(
        _in_proj_kernel,
        grid=(T // tm,),
        in_specs=[rows(D_MODEL), full((D_MODEL, aw)), full((aw, D_MODEL)), full((D_MODEL, vw)),
                  pl.BlockSpec((tm, aw), lambda i: (i % ns, 0)),
                  pl.BlockSpec((aw, tm), lambda i: (0, i % ns)),
                  full((1, vw)),
                  full((D_MODEL, 5 * BR_W)), full((D_MODEL, C_PAD_W)), full((D_MODEL, 2 * BR_W))],
        out_specs=[rows(aw),
                   pl.BlockSpec((1, aw, tm), lambda i: (i // ns, 0, i % ns)),
                   pl.BlockSpec((1, tm, vw), lambda i: (i // ns, i % ns, 0)),
                   rows(5 * BR_W), rows(C_PAD_W), rows(2 * BR_W)],
        out_shape=[jax.ShapeDtypeStruct((T, aw), BF16),
                   jax.ShapeDtypeStruct((B, aw, S), BF16),
                   jax.ShapeDtypeStruct((B, S, vw), BF16),
                   jax.ShapeDtypeStruct((T, 5 * BR_W), F32),
                   jax.ShapeDtypeStruct((T, C_PAD_W), F32),
                   jax.ShapeDtypeStruct((T, 2 * BR_W), F32)],
        compiler_params=_cparams(("parallel",)),
        name="in_proj",
    )(x2, w["aq"], w["akT"], w["av"], tabs["a_qpos"], tabs["a_kpos"], tabs["v_ones"], w["b"], w["c"], w["d"])


def _softmax_tile(n_prob, qk, v_of, c_of, corr_of, m_s, acc_s):
    s_next = qk(0)
    for p in range(n_prob):
        s = s_next
        if p + 1 < n_prob:
            s_next = qk(p + 1)
        corr = corr_of(p)
        if corr is not None:
            s = s + corr
        c = c_of(p)
        m_old = m_s[p]
        m_new = jnp.maximum(m_old, jnp.max(s, axis=1, keepdims=True) + c)
        alpha = jnp.exp(m_old - m_new)
        pr = jnp.exp(s - (m_new - c))
        acc_s[p] = alpha * acc_s[p] + _dot(pr.astype(BF16), v_of(p))
        m_s[p] = m_new


def _diff_attn_kernel(lam_ref, q_ref, kT_ref, v_ref, g_ref, o_ref, qs, m_s, acc_s, *, tq, tk, nk, lambda_init):
    qi = pl.program_id(1)
    lane = lax.broadcasted_iota(jnp.int32, (1, A_NP * A_QKW), 1)
    is_pos = (lane % A_QKW) >= A_DK
    qa = q_ref[...].astype(F32)
    qneg = jnp.where(is_pos, -qa, qa)
    for p in range(A_NP):
        qs[p] = qa[:, A_QKW * p:A_QKW * (p + 1)].astype(BF16)
        qs[A_NP + p] = qneg[:, A_QKW * p:A_QKW * (p + 1)].astype(BF16)
    m_s[...] = jnp.full(m_s.shape, -jnp.inf, F32)
    acc_s[...] = jnp.zeros(acc_s.shape, F32)

    def tile(j, variant, straddle):
        j0 = pl.multiple_of(j * tk, tk)
        off = (qi * tq - j0).astype(F32)
        if straddle:
            d = (lax.broadcasted_iota(jnp.int32, (tq, tk), 0)
                 - lax.broadcasted_iota(jnp.int32, (tq, tk), 1)).astype(F32) + off
            dpos = jnp.maximum(d, 0.0)

        def qk(p):
            return _dot(qs[variant * A_NP + p], kT_ref[0, A_QKW * p:A_QKW * (p + 1), pl.ds(j0, tk)])

        _softmax_tile(
            A_NP, qk,
            v_of=lambda p: v_ref[0, pl.ds(j0, tk), V_AUG_W * (p // 2):V_AUG_W * (p // 2 + 1)],
            c_of=lambda p: (off if variant == 0 else -off) * ALIBI_SLOPES[p // 2],
            corr_of=lambda p: dpos * (-2.0 * ALIBI_SLOPES[p // 2]) if straddle else None,
            m_s=m_s, acc_s=acc_s)

    def left(j, carry):
        tile(j, 1, False)
        return carry

    def right(j, carry):
        tile(j, 0, False)
        return carry

    jd = qi // (tk // tq)
    lax.fori_loop(0, jd, left, 0)
    tile(jd, 0, True)
    lax.fori_loop(jd + 1, nk, right, 0)

    lam = lam_ref[0]
    for h in range(A_HEADS):
        a0 = acc_s[2 * h]
        a1 = acc_s[2 * h + 1]
        o = a0[:, :A_DV] / a0[:, A_DV:A_DV + 1] - lam * (a1[:, :A_DV] / a1[:, A_DV:A_DV + 1])
        ms = jnp.mean(o * o, axis=1, keepdims=True)
        o = o * lax.rsqrt(ms + RMS_EPS) * g_ref[...] * (1.0 - lambda_init)
        o_ref[:, A_DV * h:A_DV * (h + 1)] = o.astype(BF16)


def _att_tiles(S):
    tk = _pick(S, ATT_TK)
    tq = _pick(tk, ATT_TQ)
    return tq, tk


def diff_attention(aq, akT, av, lam, subln_g, layer_idx, B, S):
    tq, tk = _att_tiles(S)
    nq = S // tq
    aw = A_NP * A_QKW
    vw = A_HEADS * V_AUG_W
    lambda_init = 0.8 - 0.6 * math.exp(-0.3 * layer_idx)
    kern = functools.partial(_diff_attn_kernel, tq=tq, tk=tk, nk=S // tk, lambda_init=lambda_init)
    return pl.pallas_call(
        kern,
        grid=(B, nq),
        in_specs=[pl.BlockSpec(memory_space=pltpu.SMEM),
                  pl.BlockSpec((tq, aw), lambda b, i: (b * nq + i, 0)),
                  pl.BlockSpec((1, aw, S), lambda b, i: (b, 0, 0)),
                  pl.BlockSpec((1, S, vw), lambda b, i: (b, 0, 0)),
                  pl.BlockSpec((1, A_DV), lambda b, i: (0, 0))],
        out_specs=pl.BlockSpec((tq, BR_W), lambda b, i: (b * nq + i, 0)),
        out_shape=jax.ShapeDtypeStruct((B * S, BR_W), BF16),
        scratch_shapes=[pltpu.VMEM((2 * A_NP, tq, A_QKW), BF16),
                        pltpu.VMEM((A_NP, tq, 1), F32),
                        pltpu.VMEM((A_NP, tq, V_AUG_W), F32)],
        compiler_params=_cparams(("parallel", "arbitrary")),
        name="diff_attn",
    )(lam, aq, akT, av, subln_g)


def _mla_prep_kernel(c_ref, qg_ref, kvg_ref, wuq, wuq_rot, wukT, wuv, vones, cosq, sinq, cosk, sink, eye_ref,
                     q_o, kT_o, v_o):
    c = c_ref[...]
    cq = c[:, :C_Q_LORA]
    ckv = c[:, C_Q_LORA:C_Q_LORA + C_KV_LORA]
    ckr = c[:, C_Q_LORA + C_KV_LORA:C_Q_LORA + C_KV_LORA + C_ROPE]
    ckr_rot = c[:, C_Q_LORA + C_KV_LORA + C_ROPE:C_Q_LORA + C_KV_LORA + 2 * C_ROPE]
    nq = (cq * lax.rsqrt(jnp.mean(cq * cq, axis=1, keepdims=True) + RMS_EPS) * qg_ref[...]).astype(BF16)
    q = _dot(nq, wuq[...]) * cosq[...] + _dot(nq, wuq_rot[...]) * sinq[...]
    q_o[...] = (q * (C_QK ** -0.5)).astype(BF16)
    nkv = (ckv * lax.rsqrt(jnp.mean(ckv * ckv, axis=1, keepdims=True) + RMS_EPS) * kvg_ref[...]).astype(BF16)
    knT = _dot_nt(wukT[...], nkv).astype(BF16)
    v_o[0] = (_dot(nkv, wuv[...]) + vones[...]).astype(BF16)
    kr = (ckr * cosk[...] + ckr_rot * sink[...]).astype(BF16)
    krT = _dot_nt(eye_ref[...], kr).astype(BF16)
    for h in range(C_HEADS):
        kT_o[0, C_QK * h:C_QK * h + C_NOPE, :] = knT[C_NOPE * h:C_NOPE * (h + 1), :]
        kT_o[0, C_QK * h + C_NOPE:C_QK * (h + 1), :] = krT


def mla_prep(c_all, w, tabs, B, S):
    T = B * S
    tm = _pick(S, 512)
    ns = S // tm
    full = lambda shape: pl.BlockSpec(shape, lambda i: (0,) * len(shape))
    rows = lambda wdt: pl.BlockSpec((tm, wdt), lambda i: (i, 0))
    pos = lambda wdt: pl.BlockSpec((tm, wdt), lambda i: (i % ns, 0))
    qw = C_HEADS * C_QK
    vw = C_HEADS * V_AUG_W
    return pl.pallas_call(
        _mla_prep_kernel,
        grid=(T // tm,),
        in_specs=[rows(C_PAD_W), full((1, C_Q_LORA)), full((1, C_KV_LORA)), full((C_Q_LORA, qw)),
                  full((C_Q_LORA, qw)), full((C_HEADS * C_NOPE, C_KV_LORA)), full((C_KV_LORA, vw)),
                  full((1, vw)), pos(qw), pos(qw), pos(C_ROPE), pos(C_ROPE), full((C_ROPE, C_ROPE))],
        out_specs=[rows(qw),
                   pl.BlockSpec((1, qw, tm), lambda i: (i // ns, 0, i % ns)),
                   pl.BlockSpec((1, tm, vw), lambda i: (i // ns, i % ns, 0))],
        out_shape=[jax.ShapeDtypeStruct((T, qw), BF16),
                   jax.ShapeDtypeStruct((B, qw, S), BF16),
                   jax.ShapeDtypeStruct((B, S, vw), BF16)],
        compiler_params=_cparams(("parallel",)),
        name="mla_prep",
    )(c_all, w["q_norm_g"], w["kv_norm_g"], w["wuq"], w["wuq_rot"], w["wukT"], w["wuv"], tabs["v_ones"],
      tabs["cosq"], tabs["sinq"], tabs["cosk"], tabs["sink"], tabs["eye"])


def _mla_attn_kernel(q_ref, kT_ref, v_ref, o_ref, qs, m_s, acc_s, *, tk, nk):
    for h in range(C_HEADS):
        qs[h] = q_ref[:, C_QK * h:C_QK * (h + 1)]
    m_s[...] = jnp.full(m_s.shape, -jnp.inf, F32)
    acc_s[...] = jnp.zeros(acc_s.shape, F32)

    def tile(j, carry):
        j0 = pl.multiple_of(j * tk, tk)
        _softmax_tile(
            C_HEADS,
            qk=lambda h: _dot(qs[h], kT_ref[0, C_QK * h:C_QK * (h + 1), pl.ds(j0, tk)]),
            v_of=lambda h: v_ref[0, pl.ds(j0, tk), V_AUG_W * h:V_AUG_W * (h + 1)],
            c_of=lambda h: 0.0,
            corr_of=lambda h: None,
            m_s=m_s, acc_s=acc_s)
        return carry

    lax.fori_loop(0, nk, tile, 0)
    for h in range(C_HEADS):
        a = acc_s[h]
        o_ref[:, C_V * h:C_V * (h + 1)] = (a[:, :C_V] / a[:, C_V:C_V + 1]).astype(BF16)


def mla_attention(q, kT, v, B, S):
    tq, tk = _att_tiles(S)
    nq = S // tq
    qw = C_HEADS * C_QK
    vw = C_HEADS * V_AUG_W
    kern = functools.partial(_mla_attn_kernel, tk=tk, nk=S // tk)
    return pl.pallas_call(
        kern,
        grid=(B, nq),
        in_specs=[pl.BlockSpec((tq, qw), lambda b, i: (b * nq + i, 0)),
                  pl.BlockSpec((1, qw, S), lambda b, i: (b, 0, 0)),
                  pl.BlockSpec((1, S, vw), lambda b, i: (b, 0, 0))],
        out_specs=pl.BlockSpec((tq, BR_W), lambda b, i: (b * nq + i, 0)),
        out_shape=jax.ShapeDtypeStruct((B * S, BR_W), BF16),
        scratch_shapes=[pltpu.VMEM((C_HEADS, tq, C_QK), BF16),
                        pltpu.VMEM((C_HEADS, tq, 1), F32),
                        pltpu.VMEM((C_HEADS, tq, V_AUG_W), F32)],
        compiler_params=_cparams(("parallel", "arbitrary")),
        name="mla_attn",
    )(q, kT, v)


HGRN_LEVELS = (64, 32, 16, 8, 4, 2, 1)
N_HGRN_SUMS = len(HGRN_LEVELS) + 2


def _hgrn_constants():
    C = CHUNK
    t = np.arange(C)
    sums = np.zeros((2, N_HGRN_SUMS, C, C), np.float32)
    masks = np.zeros((2, len(HGRN_LEVELS) + 1, C, C), np.float32)
    for li, m in enumerate(HGRN_LEVELS):
        blk = t // m
        for tt in range(C):
            if blk[tt] % 2 == 1:
                sums[0, li, tt, blk[tt] * m:tt + 1] = 1.0
            else:
                sums[0, li, tt, tt + 1:(blk[tt] + 1) * m] = 1.0
        masks[0, li] = ((blk[:, None] % 2 == 1) & (blk[None, :] == blk[:, None] - 1)).astype(np.float32)
    masks[0, len(HGRN_LEVELS)] = np.eye(C, dtype=np.float32)
    sums[0, len(HGRN_LEVELS)] = (t[None, :] <= t[:, None]).astype(np.float32)
    sums[0, len(HGRN_LEVELS) + 1] = (t[None, :] > t[:, None]).astype(np.float32)
    sums[1] = sums[0][:, ::-1, ::-1]
    masks[1] = masks[0][:, ::-1, ::-1]
    return sums.reshape(2, N_HGRN_SUMS * C, C), masks


def _hgrn_kernel(qf_ref, ff_ref, vf_ref, qb_ref, fb_ref, vb_ref, lb_ref, sums_ref, mask_ref,
                 of_ref, ob_ref, st_ref):
    C = CHUNK
    nlev = len(HGRN_LEVELS)

    @pl.when(pl.program_id(1) == 0)
    def _():
        st_ref[...] = jnp.zeros(st_ref.shape, F32)

    lane = lax.broadcasted_iota(jnp.int32, (1, 2 * B_DK), 1)
    head_lanes = (lane < B_DK, lane >= B_DK)
    r_i = lax.broadcasted_iota(jnp.int32, (2 * B_DK, 2 * B_DK), 0)
    c_i = lax.broadcasted_iota(jnp.int32, (2 * B_DK, 2 * B_DK), 1)
    same_head = (r_i < B_DK) == (c_i < B_DK)

    for d, (q_ref, f_ref, v_ref, o_ref) in enumerate(((qf_ref, ff_ref, vf_ref, of_ref),
                                                      (qb_ref, fb_ref, vb_ref, ob_ref))):
        x = q_ref[...]
        q = x * _sigmoid(x)
        lb = lb_ref[d]
        f = lb + (1.0 - lb) * _sigmoid(f_ref[...])
        logf = jnp.log(f)
        k = 1.0 - f
        v = v_ref[...]
        hi = logf.astype(BF16)
        lo = (logf - hi.astype(F32)).astype(BF16)
        sums = sums_ref[d]
        E = _dot(sums, hi) + _dot(sums, lo)
        for p in range(B_HEADS // 2):
            sl = slice(2 * B_DK * p, 2 * B_DK * (p + 1))
            Qp, Kp, Vp = q[:, sl], k[:, sl], v[:, sl]
            A = [jnp.zeros((C, C), F32), jnp.zeros((C, C), F32)]
            for l in range(nlev + 1):
                if l < nlev:
                    G = jnp.exp(E[C * l:C * (l + 1), sl])
                    QG = (Qp * G).astype(BF16)
                    KG = Kp * G
                else:
                    QG = Qp.astype(BF16)
                    KG = Kp
                for h in range(2):
                    KGh = jnp.where(head_lanes[h], KG, 0.0).astype(BF16)
                    A[h] = A[h] + mask_ref[d, l] * _dot_nt(QG, KGh)
            o = (_dot(A[0].astype(BF16), jnp.where(head_lanes[0], Vp, 0.0).astype(BF16))
                 + _dot(A[1].astype(BF16), jnp.where(head_lanes[1], Vp, 0.0).astype(BF16)))
            eb = jnp.exp(E[C * nlev:C * (nlev + 1), sl])
            st = st_ref[d, p]
            o = o + _dot_nt((Qp * eb).astype(BF16), st.astype(BF16))
            o_ref[:, sl] = o
            Kd = (Kp * jnp.exp(E[C * (nlev + 1):C * (nlev + 2), sl])).astype(BF16)
            row = C - 1 if d == 0 else 0
            g = eb[row:row + 1, :]
            upd = _dot_tn(Vp.astype(BF16), Kd)
            st_ref[d, p] = st * g + jnp.where(same_head, upd, 0.0)


def hgrn2(b_all, lb, B, S):
    T = B * S
    C = CHUNK
    nc = S // C
    sums, masks = _hgrn_constants()
    sums = jnp.asarray(sums, BF16)
    masks = jnp.asarray(masks, F32)
    fwd = lambda col: pl.BlockSpec((C, BR_W), lambda b, n: (b * nc + n, col))
    bwd = lambda col: pl.BlockSpec((C, BR_W), lambda b, n: (b * nc + nc - 1 - n, col))
    full = lambda shape: pl.BlockSpec(shape, lambda b, n: (0,) * len(shape))
    return pl.pallas_call(
        _hgrn_kernel,
        grid=(B, nc),
        in_specs=[fwd(0), fwd(1), fwd(3), bwd(0), bwd(2), bwd(3),
                  full((2, 1, BR_W)), full(sums.shape), full(masks.shape)],
        out_specs=[pl.BlockSpec((C, BR_W), lambda b, n: (b * nc + n, 0)),
                   pl.BlockSpec((C, BR_W), lambda b, n: (b * nc + nc - 1 - n, 0))],
        out_shape=[jax.ShapeDtypeStruct((T, BR_W), F32), jax.ShapeDtypeStruct((T, BR_W), F32)],
        scratch_shapes=[pltpu.VMEM((2, B_HEADS // 2, 2 * B_DK, 2 * B_DK), F32)],
        compiler_params=_cparams(("parallel", "arbitrary")),
        name="hgrn2",
    )(b_all, b_all, b_all, b_all, b_all, b_all, lb, sums, masks)


HALO = 8


def _rglru_kernel(xf_ref, xfp_ref, xfn_ref, xb_ref, xbp_ref, xbn_ref, cw_ref, cb_ref, wa_ref, ba_ref,
                  wx_ref, bx_ref, sp_ref, hf_ref, hb_ref, carry_ref, *, tc):
    n = pl.program_id(1)
    nc = pl.num_programs(1)

    @pl.when(n == 0)
    def _():
        carry_ref[...] = jnp.zeros(carry_ref.shape, F32)

    row = lax.broadcasted_iota(jnp.int32, (tc, 1), 0)
    for d, (x_ref, xp_ref, xn_ref, o_ref) in enumerate(((xf_ref, xfp_ref, xfn_ref, hf_ref),
                                                        (xb_ref, xbp_ref, xbn_ref, hb_ref))):
        cidx = n if d == 0 else nc - 1 - n
        prev = jnp.where(cidx > 0, xp_ref[...], 0.0)
        nxt = jnp.where(cidx < nc - 1, xn_ref[...], 0.0)
        ext = jnp.concatenate([prev, x_ref[...], nxt], axis=0)
        xc = cb_ref[...]
        for j in range(CONV_W):
            off = HALO - CONV_LEFT + j
            xc = xc + ext[off:off + tc, :] * cw_ref[j:j + 1, :]
        xcb = xc.astype(BF16)
        r = _sigmoid(_dot(xcb, wa_ref[d]) + ba_ref[d])
        i = _sigmoid(_dot(xcb, wx_ref[d]) + bx_ref[d])
        log_a = (-RG_C) * r * sp_ref[d]
        a = jnp.exp(log_a)
        u = jnp.sqrt(1.0 - jnp.exp(2.0 * log_a)) * (i * xc)
        sh = 1
        while sh < tc:
            if d == 0:
                valid = row >= sh
                amt = sh
            else:
                valid = row < tc - sh
                amt = tc - sh
            a_sh = jnp.where(valid, pltpu.roll(a, amt, 0), 1.0)
            u_sh = jnp.where(valid, pltpu.roll(u, amt, 0), 0.0)
            u = a * u_sh + u
            a = a * a_sh
            sh *= 2
        h = u + a * carry_ref[d]
        o_ref[...] = h
        last = tc - 1 if d == 0 else 0
        carry_ref[d] = h[last:last + 1, :]


def rglru(d_all, w, B, S):
    T = B * S
    tc = _pick(S, 512)
    nc = S // tc
    hb = tc // HALO
    nrow8 = T // HALO
    fc = lambda b, n: b * nc + n
    bc = lambda b, n: b * nc + nc - 1 - n
    cur = lambda f: pl.BlockSpec((tc, BR_W), lambda b, n: (f(b, n), 0))
    prv = lambda f: pl.BlockSpec((HALO, BR_W), lambda b, n: (jnp.maximum(f(b, n) * hb - 1, 0), 0))
    nxt = lambda f: pl.BlockSpec((HALO, BR_W), lambda b, n: (jnp.minimum((f(b, n) + 1) * hb, nrow8 - 1), 0))
    full = lambda shape: pl.BlockSpec(shape, lambda b, n: (0,) * len(shape))
    kern = functools.partial(_rglru_kernel, tc=tc)
    return pl.pallas_call(
        kern,
        grid=(B, nc),
        in_specs=[cur(fc), prv(fc), nxt(fc), cur(bc), prv(bc), nxt(bc),
                  full((CONV_W, BR_W)), full((1, BR_W)), full((2, BR_W, BR_W)), full((2, 1, BR_W)),
                  full((2, BR_W, BR_W)), full((2, 1, BR_W)), full((2, 1, BR_W))],
        out_specs=[cur(fc), cur(bc)],
        out_shape=[jax.ShapeDtypeStruct((T, BR_W), F32), jax.ShapeDtypeStruct((T, BR_W), F32)],
        scratch_shapes=[pltpu.VMEM((2, 1, BR_W), F32)],
        compiler_params=_cparams(("parallel", "arbitrary")),
        name="rglru",
    )(d_all, d_all, d_all, d_all, d_all, d_all, w["conv_w"], w["conv_b"], w["wa"], w["ba"], w["wx"],
      w["bx"], w["sp"])


def _layer_norm(z, g, b):
    mu = jnp.mean(z, axis=1, keepdims=True)
    zc = z - mu
    var = jnp.mean(zc * zc, axis=1, keepdims=True)
    return zc * lax.rsqrt(var + LN_EPS) * g + b


def _merge_kernel(x_ref, ya_ref, of_ref, ob_ref, bg_ref, yc_ref, hf_ref, hb_ref, dg_ref, wg_ref, wbr_ref,
                  wo_ref, lng_ref, lnb_ref, hg_ref, havg_ref, wr_ref, x1_ref, aff_ref):
    x = x_ref[...]
    xb = x.astype(BF16)
    o = of_ref[...] + ob_ref[...]
    ms = _dot_split(o * o, havg_ref[...])
    bg = bg_ref[...]
    yb = o * lax.rsqrt(ms + RMS_EPS) * hg_ref[...] * (bg * _sigmoid(bg))
    dg = dg_ref[...]
    gelu = 0.5 * dg * (1.0 + jnp.tanh(0.7978845608028654 * (dg + 0.044715 * dg * dg * dg)))
    yd = (hf_ref[...] + hb_ref[...]) * gelu
    ys = (ya_ref[...], yb.astype(BF16), yc_ref[...], yd.astype(BF16))
    mix = None
    for i in range(N_BRANCH):
        gate = _sigmoid(_dot(xb, wg_ref[:, D_MODEL * i:D_MODEL * (i + 1)]))
        term = gate * _dot(ys[i], wbr_ref[i])
        mix = term if mix is None else mix + term
    z = ALPHA * x + _dot(mix.astype(BF16), wo_ref[...])
    x1 = _layer_norm(z, lng_ref[...], lnb_ref[...])
    x1_ref[...] = x1
    x1h = x1.astype(BF16)
    x1l = (x1 - x1h.astype(F32)).astype(BF16)
    logits = _dot_nt(x1h, wr_ref[0]) + (_dot_nt(x1l, wr_ref[0]) + _dot_nt(x1h, wr_ref[1]))
    e = jnp.exp(logits - jnp.max(logits, axis=1, keepdims=True))
    aff_ref[...] = e / jnp.sum(e, axis=1, keepdims=True)


def merge(x2, ya, of, ob, b_all, yc, hf, hb, d_all, w):
    T = x2.shape[0]
    tm = _pick(T, 512)
    full = lambda shape: pl.BlockSpec(shape, lambda i: (0,) * len(shape))
    rows = lambda wdt: pl.BlockSpec((tm, wdt), lambda i: (i, 0))
    colblk = lambda c: pl.BlockSpec((tm, BR_W), lambda i: (i, c))
    return pl.pallas_call(
        _merge_kernel,
        grid=(T // tm,),
        in_specs=[rows(D_MODEL), rows(BR_W), rows(BR_W), rows(BR_W), colblk(4), rows(BR_W), rows(BR_W),
                  rows(BR_W), colblk(1), full((D_MODEL, N_BRANCH * D_MODEL)),
                  full((N_BRANCH, BR_W, D_MODEL)), full((D_MODEL, D_MODEL)), full((1, D_MODEL)),
                  full((1, D_MODEL)), full((1, BR_W)), full((BR_W, BR_W)), full((2, N_EXPERTS, D_MODEL))],
        out_specs=[rows(D_MODEL), rows(N_EXPERTS)],
        out_shape=[jax.ShapeDtypeStruct((T, D_MODEL), F32), jax.ShapeDtypeStruct((T, N_EXPERTS), F32)],
        compiler_params=_cparams(("parallel",)),
        name="merge",
    )(x2, ya, of, ob, b_all, yc, hf, hb, d_all, w["gate"], w["branch"], w["out"], w["ln_g"], w["ln_b"],
      w["hgrn_norm_g"], w["head_avg"], w["router"])


def _select_kernel(aff_ref, sel_ref, *, cap, n_tok):
    aff = aff_ref[...]
    v = lax.bitcast_convert_type(aff, jnp.int32)
    E, R, L = aff.shape

    def count(pred):
        c = jnp.sum(pred.astype(F32), axis=1)
        return jnp.sum(c, axis=1, keepdims=True)

    def value_step(i, thr):
        cand = thr | jnp.left_shift(jnp.int32(1), 30 - i)
        ok = count(v >= cand[:, :, None]) >= cap
        return jnp.where(ok, cand, thr)

    thr = lax.fori_loop(0, 31, value_step, jnp.zeros((E, 1), jnp.int32))
    thr3 = thr[:, :, None]
    gt = v > thr3
    eq = v == thr3
    need = cap - count(gt)
    tok = (lax.broadcasted_iota(jnp.int32, (E, R, L), 1) * L
           + lax.broadcasted_iota(jnp.int32, (E, R, L), 2))
    nbits = max(1, int(n_tok).bit_length())

    def index_step(i, bound):
        cand = bound | jnp.left_shift(jnp.int32(1), nbits - 1 - i)
        ok = count(eq & (tok < cand[:, :, None])) <= need
        return jnp.where(ok, cand, bound)

    bound = lax.fori_loop(0, nbits, index_step, jnp.zeros((E, 1), jnp.int32))
    sel = gt | (eq & (tok < bound[:, :, None]))
    sel_ref[...] = sel.astype(jnp.int32)


def select_tokens(affT3, cap, n_tok):
    E, R, L = affT3.shape
    kern = functools.partial(_select_kernel, cap=cap, n_tok=n_tok)
    return pl.pallas_call(
        kern,
        grid=(1,),
        in_specs=[pl.BlockSpec((E, R, L), lambda i: (0, 0, 0))],
        out_specs=pl.BlockSpec((E, R, L), lambda i: (0, 0, 0)),
        out_shape=jax.ShapeDtypeStruct((E, R, L), jnp.int32),
        compiler_params=_cparams(("arbitrary",)),
        name="select_tokens",
    )(affT3)


FFN_FCHUNK = 512


def _ffn_kernel(idx_hbm, x_hbm, g_ref, wg_ref, wu_ref, wd_ref, ye_ref, idx_smem, xbuf, idx_sem, row_sem,
                *, tm, n_blocks):
    blk = pl.program_id(0) * pl.num_programs(1) + pl.program_id(1)

    last = n_blocks - 1
    slot = blk % 2
    nslot = 1 - slot

    def idx_copy(b, s):
        return pltpu.make_async_copy(idx_hbm.at[b], idx_smem.at[s], idx_sem.at[s])

    def row_copy(src_row, s, j):
        return pltpu.make_async_copy(x_hbm.at[pl.ds(src_row, 1), :], xbuf.at[s, pl.ds(j, 1), :],
                                     row_sem.at[s])

    def wait_rows(s):
        pltpu.make_async_copy(x_hbm.at[pl.ds(0, tm), :], xbuf.at[s], row_sem.at[s]).wait()

    @pl.when(blk == 0)
    def _():
        first = idx_copy(0, 0)
        first.start()
        first.wait()

        def issue(j, carry):
            row_copy(idx_smem[0, j], 0, j).start()
            return carry

        lax.fori_loop(0, tm, issue, 0)
        idx_copy(jnp.minimum(1, last), 1).start()

    idx_copy(0, nslot).wait()
    wait_rows(slot)
    xb = xbuf[slot].astype(BF16)
    acc = jnp.zeros((tm, D_MODEL), F32)
    n_chunks = D_EXPERT // FFN_FCHUNK
    per_chunk = tm // n_chunks
    for c in range(n_chunks):
        for j in range(per_chunk * c, per_chunk * (c + 1)):
            row_copy(idx_smem[nslot, j], nslot, j).start()
        fs = slice(FFN_FCHUNK * c, FFN_FCHUNK * (c + 1))
        hg = _dot(xb, wg_ref[0, :, fs])
        hu = _dot(xb, wu_ref[0, :, fs])
        h = (hg * _sigmoid(hg) * hu).astype(BF16)
        acc = acc + _dot(h, wd_ref[0, fs, :])
    ye_ref[...] = acc * g_ref[...]
    idx_copy(jnp.minimum(blk + 2, last), slot).start()

    @pl.when(blk == last)
    def _():
        wait_rows(nslot)
        idx_copy(0, slot).wait()


def expert_ffn(idx, x1, g, wg, wu, wd):
    E, cap = idx.shape
    tm = _pick(cap, 512)
    R = cap // tm
    kern = functools.partial(_ffn_kernel, tm=tm, n_blocks=E * R)
    return pl.pallas_call(
        kern,
        grid=(E, R),
        in_specs=[pl.BlockSpec(memory_space=pl.ANY),
                  pl.BlockSpec(memory_space=pl.ANY),
                  pl.BlockSpec((tm, 1), lambda e, r: (e * R + r, 0)),
                  pl.BlockSpec((1, D_MODEL, D_EXPERT), lambda e, r: (e, 0, 0)),
                  pl.BlockSpec((1, D_MODEL, D_EXPERT), lambda e, r: (e, 0, 0)),
                  pl.BlockSpec((1, D_EXPERT, D_MODEL), lambda e, r: (e, 0, 0))],
        out_specs=pl.BlockSpec((tm, D_MODEL), lambda e, r: (e * R + r, 0)),
        out_shape=jax.ShapeDtypeStruct((E * cap, D_MODEL), F32),
        scratch_shapes=[pltpu.SMEM((2, tm), jnp.int32),
                        pltpu.VMEM((2, tm, D_MODEL), F32),
                        pltpu.SemaphoreType.DMA((2,)),
                        pltpu.SemaphoreType.DMA((2,))],
        compiler_params=pltpu.CompilerParams(dimension_semantics=("arbitrary", "arbitrary"),
                                             vmem_limit_bytes=VMEM_LIMIT, disable_bounds_checks=True),
        name="expert_ffn",
    )(idx.reshape(E * R, tm), x1, g.reshape(E * cap, 1), wg, wu, wd)


CMB_TT = 128
CMB_PIECE = 32
CMB_CHUNK = 8
SUBLANES = 8


def _combine_kernel(p_ref, ye_hbm, sel_ref, x_ref, tri_ref, g_ref, b_ref, o_ref,
                    buf, meta, nfill, pre_ref, oh_ref, acc_ref, sem, *, cap, n_rows, n_tiles, max_pieces):
    i = pl.program_id(0)
    slot = i % 2
    E = N_EXPERTS
    PC = CMB_PIECE

    def piece_copy(src_row, s, k):
        return pltpu.make_async_copy(ye_hbm.at[pl.ds(src_row, PC), :], buf.at[s, pl.ds(k * PC, PC), :],
                                     sem.at[s])

    def issue_tile(tile, s):
        def expert_body(e, fill):
            lo = p_ref[tile * E + e]
            s0 = e * cap + lo
            s1 = e * cap + p_ref[(tile + 1) * E + e]
            a0 = (s0 // SUBLANES) * SUBLANES

            def piece(q, fill):
                start = a0 + PC * q
                start_c = pl.multiple_of(jnp.minimum(start, n_rows - PC), SUBLANES)
                piece_copy(start_c, s, fill).start()
                meta[s, 0, fill] = e
                meta[s, 1, fill] = start_c - s0
                meta[s, 2, fill] = start - start_c
                return fill + 1

            n_pieces = jnp.where(s1 > s0, (s1 - a0 + PC - 1) // PC, 0)
            return lax.fori_loop(0, n_pieces, piece, fill)

        nfill[s] = lax.fori_loop(0, E, expert_body, 0)

    @pl.when(i == 0)
    def _():
        buf[...] = jnp.zeros(buf.shape, F32)

        def clear(k, carry):
            for s in range(2):
                for f in range(3):
                    meta[s, f, k] = 0
            return carry

        lax.fori_loop(0, max_pieces, clear, 0)
        issue_tile(0, 0)

    @pl.when(i + 1 < n_tiles)
    def _():
        issue_tile(i + 1, 1 - slot)

    fill = nfill[slot]

    def wait_piece(k, carry):
        piece_copy(0, slot, k).wait()
        return carry

    lax.fori_loop(0, fill, wait_piece, 0)

    selv = sel_ref[...]
    pre_ref[...] = _dot(selv.astype(BF16), tri_ref[...])
    acc_ref[...] = jnp.zeros(acc_ref.shape, F32)
    row = lax.broadcasted_iota(jnp.int32, (PC, 1), 0)

    def chunk(c, carry):
        for k in range(CMB_CHUNK):
            pidx = c * CMB_CHUNK + k
            live = pidx < fill
            e = jnp.where(live, meta[slot, 0, pidx], 0)
            rank = jnp.where((row >= meta[slot, 2, pidx]) & live, row + meta[slot, 1, pidx], -1).astype(F32)
            hit = (pre_ref[pl.ds(e, 1), :] == rank) & (sel_ref[pl.ds(e, 1), :] > 0.5)
            oh_ref[PC * k:PC * (k + 1), :] = jnp.where(hit, 1.0, 0.0).astype(BF16)
        rows = buf[slot, pl.ds(pl.multiple_of(c * (CMB_CHUNK * PC), CMB_CHUNK * PC), CMB_CHUNK * PC), :]
        hi = rows.astype(BF16)
        lo = (rows - hi.astype(F32)).astype(BF16)
        oh = oh_ref[...]
        acc_ref[...] += _dot_tn(oh, hi) + _dot_tn(oh, lo)
        return carry

    lax.fori_loop(0, (fill + CMB_CHUNK - 1) // CMB_CHUNK, chunk, 0)
    o_ref[...] = _layer_norm(ALPHA * x_ref[...] + acc_ref[...], g_ref[...], b_ref[...])


def combine_norm(ye, sel_f, tile_prefix, x1, g, b, cap):
    T = x1.shape[0]
    tt = _pick(T, CMB_TT)
    n_tiles = T // tt
    max_pieces = N_EXPERTS * (tt // CMB_PIECE + 1)
    max_pieces = ((max_pieces + CMB_CHUNK - 1) // CMB_CHUNK) * CMB_CHUNK
    tri = jnp.asarray(np.triu(np.ones((tt, tt), np.float32), 1), BF16)
    kern = functools.partial(_combine_kernel, cap=cap, n_rows=ye.shape[0], n_tiles=n_tiles,
                             max_pieces=max_pieces)
    grid_spec = pltpu.PrefetchScalarGridSpec(
        num_scalar_prefetch=1,
        grid=(n_tiles,),
        in_specs=[pl.BlockSpec(memory_space=pl.ANY),
                  pl.BlockSpec((N_EXPERTS, tt), lambda i, p: (0, i)),
                  pl.BlockSpec((tt, D_MODEL), lambda i, p: (i, 0)),
                  pl.BlockSpec((tt, tt), lambda i, p: (0, 0)),
                  pl.BlockSpec((1, D_MODEL), lambda i, p: (0, 0)),
                  pl.BlockSpec((1, D_MODEL), lambda i, p: (0, 0))],
        out_specs=pl.BlockSpec((tt, D_MODEL), lambda i, p: (i, 0)),
        scratch_shapes=[pltpu.VMEM((2, max_pieces * CMB_PIECE, D_MODEL), F32),
                        pltpu.SMEM((2, 3, max_pieces), jnp.int32),
                        pltpu.SMEM((2,), jnp.int32),
                        pltpu.VMEM((N_EXPERTS, tt), F32),
                        pltpu.VMEM((CMB_CHUNK * CMB_PIECE, tt), BF16),
                        pltpu.VMEM((tt, D_MODEL), F32),
                        pltpu.SemaphoreType.DMA((2,))])
    return pl.pallas_call(
        kern,
        grid_spec=grid_spec,
        out_shape=jax.ShapeDtypeStruct((T, D_MODEL), F32),
        compiler_params=pltpu.CompilerParams(dimension_semantics=("arbitrary",), vmem_limit_bytes=VMEM_LIMIT,
                                             disable_bounds_checks=True),
        name="combine_norm",
    )(tile_prefix, ye, sel_f, x1, tri, g, b)


def _rot_cols(w, start):
    half = C_ROPE // 2
    return jnp.concatenate([-w[:, start + half:start + C_ROPE], w[:, start:start + half]], axis=1)


def _block_diag(w):
    n, c, _ = w.shape
    out = jnp.zeros((n * c, n * c), w.dtype)
    for i in range(n):
        out = out.at[i * c:(i + 1) * c, i * c:(i + 1) * c].set(w[i])
    return out


def _prep_layer(l, p, lb_all):
    w_in = p["w_in"][l]
    o = 0
    aq = w_in[:, o:o + BR_W]; o += BR_W
    ak = w_in[:, o:o + BR_W]; o += BR_W
    av = w_in[:, o:o + BR_W]; o += BR_W
    wb = w_in[:, o:o + 5 * BR_W]; o += 5 * BR_W
    c0 = o
    wcq = w_in[:, o:o + C_Q_LORA]; o += C_Q_LORA
    wckv = w_in[:, o:o + C_KV_LORA]; o += C_KV_LORA
    wckr = w_in[:, o:o + C_ROPE]; o += C_ROPE
    wd = w_in[:, o:o + 2 * BR_W]; o += 2 * BR_W
    wgate = w_in[:, o:o + N_BRANCH * D_MODEL]
    used = C_Q_LORA + C_KV_LORA + 2 * C_ROPE
    wc = jnp.concatenate([wcq, wckv, wckr, _rot_cols(w_in, c0 + C_Q_LORA + C_KV_LORA),
                          jnp.zeros((D_MODEL, C_PAD_W - used), F32)], axis=1)
    padq = jnp.zeros((D_MODEL, A_QKW - A_DK), F32)
    padv = jnp.zeros((D_MODEL, V_AUG_W - A_DV), F32)
    aq_w = jnp.concatenate([t for p_ in range(A_NP) for t in (aq[:, A_DK * p_:A_DK * (p_ + 1)], padq)], axis=1)
    ak_w = jnp.concatenate([t for p_ in range(A_NP) for t in (ak[:, A_DK * p_:A_DK * (p_ + 1)], padq)], axis=1)
    av_w = jnp.concatenate([t for h in range(A_HEADS) for t in (av[:, A_DV * h:A_DV * (h + 1)], padv)], axis=1)
    proj = dict(aq=aq_w.astype(BF16), akT=ak_w.T.astype(BF16), av=av_w.astype(BF16), b=wb.astype(BF16),
                c=wc.astype(BF16), d=wd.astype(BF16))

    wuq = p["mla_w_uq"][l]
    wuq_rot = jnp.zeros_like(wuq)
    for h in range(C_HEADS):
        r0 = C_QK * h + C_NOPE
        wuq_rot = wuq_rot.at[:, r0:r0 + C_ROPE].set(_rot_cols(wuq, r0))
    wukv = p["mla_w_ukv"][l].reshape(C_KV_LORA, C_HEADS, C_NOPE + C_V)
    wuv = jnp.concatenate([wukv[:, :, C_NOPE:], jnp.zeros((C_KV_LORA, C_HEADS, V_AUG_W - C_V), F32)], axis=2)
    mla = dict(q_norm_g=p["mla_q_norm_g"][l][None, :], kv_norm_g=p["mla_kv_norm_g"][l][None, :],
               wuq=wuq.astype(BF16), wuq_rot=wuq_rot.astype(BF16),
               wukT=wukv[:, :, :C_NOPE].reshape(C_KV_LORA, C_HEADS * C_NOPE).T.astype(BF16),
               wuv=wuv.reshape(C_KV_LORA, C_HEADS * V_AUG_W).astype(BF16))

    rg = dict(conv_w=p["rg_conv_w"][l], conv_b=p["rg_conv_b"][l][None, :],
              wa=jnp.stack([_block_diag(p["rg_w_a"][l, d]) for d in range(2)]).astype(BF16),
              ba=p["rg_b_a"][l][:, None, :],
              wx=jnp.stack([_block_diag(p["rg_w_x"][l, d]) for d in range(2)]).astype(BF16),
              bx=p["rg_b_x"][l][:, None, :],
              sp=jax.nn.softplus(-p["rg_lambda"][l].astype(F32))[:, None, :])

    head = np.arange(BR_W) // B_DK
    wr_hi = p["w_router"][l].astype(BF16)
    mrg = dict(gate=wgate.astype(BF16), branch=p["w_branch"][l].astype(BF16), out=p["w_out"][l].astype(BF16),
               ln_g=p["ln_g"][l, 0][None, :], ln_b=p["ln_b"][l, 0][None, :],
               hgrn_norm_g=p["hgrn_norm_g"][l][None, :],
               head_avg=jnp.asarray((head[:, None] == head[None, :]).astype(np.float32) / B_DK, BF16),
               router=jnp.stack([wr_hi.T, (p["w_router"][l] - wr_hi.astype(F32)).astype(BF16).T]))

    lp = p["diff_lambda"][l].astype(F32)
    lambda_init = 0.8 - 0.6 * math.exp(-0.3 * l)
    lam = (jnp.exp(jnp.sum(lp[0] * lp[1])) - jnp.exp(jnp.sum(lp[2] * lp[3])) + lambda_init).reshape(1)
    return dict(proj=proj, mla=mla, rg=rg, mrg=mrg, lam=lam, subln_g=p["diff_subln_g"][l][None, :],
                lb=lb_all[l][:, None, :],
                ffn=(p["w_e_gate"][l].astype(BF16), p["w_e_up"][l].astype(BF16), p["w_e_down"][l].astype(BF16)),
                ln2_g=p["ln_g"][l, 1][None, :], ln2_b=p["ln_b"][l, 1][None, :])


def _rope_tables(S):
    half = C_ROPE // 2
    inv = ROPE_BASE ** (-jnp.arange(0, C_ROPE, 2, dtype=F32) / C_ROPE)
    ang = jnp.arange(S, dtype=F32)[:, None] * inv[None, :]
    cos, sin = jnp.cos(ang), jnp.sin(ang)
    cosk = jnp.concatenate([cos, cos], axis=1)
    sink = jnp.concatenate([sin, sin], axis=1)
    ones = jnp.ones((S, C_NOPE), F32)
    zeros = jnp.zeros((S, C_NOPE), F32)
    cosq = jnp.concatenate([jnp.concatenate([ones, cosk], axis=1)] * C_HEADS, axis=1)
    sinq = jnp.concatenate([jnp.concatenate([zeros, sink], axis=1)] * C_HEADS, axis=1)
    tq, tk = _att_tiles(S)
    pos = jnp.arange(S, dtype=jnp.int32)
    i_rel = (pos % tq).astype(F32)
    j_rel = pos % tk
    j_hi = (4 * (j_rel // 4)).astype(F32)
    j_lo = (j_rel % 4).astype(F32)
    one = jnp.ones((S,), F32)
    qcols = jnp.stack([i_rel, one, one], axis=1)
    qpad = jnp.zeros((S, A_QKW - A_DK - 3), F32)
    qblock = jnp.concatenate([jnp.zeros((S, A_DK), F32), qcols, qpad], axis=1)
    a_qpos = jnp.concatenate([qblock] * A_NP, axis=1)
    kblocks = []
    for p_ in range(A_NP):
        sl = ALIBI_SLOPES[p_ // 2]
        krows = jnp.stack([sl * one, -sl * j_hi, -sl * j_lo], axis=0)
        kblocks += [jnp.zeros((A_DK, S), F32), krows, jnp.zeros((A_QKW - A_DK - 3, S), F32)]
    a_kpos = jnp.concatenate(kblocks, axis=0)
    v_ones = jnp.zeros((1, A_HEADS * V_AUG_W), F32).at[0, A_DV::V_AUG_W].set(1.0)
    return dict(cosq=cosq, sinq=sinq, cosk=cosk, sink=sink, eye=jnp.eye(C_ROPE, dtype=BF16),
                a_qpos=a_qpos, a_kpos=a_kpos, v_ones=v_ones)


def _layer(x2, B, S, l, w, tabs):
    T = B * S
    aq, akT, av, b_all, c_all, d_all = in_proj(x2, B, S, w["proj"], tabs)
    ya = diff_attention(aq, akT, av, w["lam"], w["subln_g"], l, B, S)
    of, ob = hgrn2(b_all, w["lb"], B, S)
    cq, ckT, cv = mla_prep(c_all, w["mla"], tabs, B, S)
    yc = mla_attention(cq, ckT, cv, B, S)
    hf, hb = rglru(d_all, w["rg"], B, S)
    x1, aff = merge(x2, ya, of, ob, b_all, yc, hf, hb, d_all, w["mrg"])

    cap = CAPACITY_FACTOR * T // N_EXPERTS
    affT = aff.T
    sel = select_tokens(affT.reshape(N_EXPERTS, T // 128, 128), cap, T).reshape(N_EXPERTS, T)
    order = jnp.argsort(1 - sel, axis=1, stable=True)[:, :cap].astype(jnp.int32)
    g = jnp.take_along_axis(affT, order, axis=1)
    ye = expert_ffn(order, x1, g, *w["ffn"])
    tt = _pick(T, CMB_TT)
    counts = jnp.sum(sel.reshape(N_EXPERTS, T // tt, tt), axis=2)
    tile_prefix = jnp.concatenate([jnp.zeros((N_EXPERTS, 1), jnp.int32), jnp.cumsum(counts, axis=1)], axis=1)
    return combine_norm(ye, sel.astype(F32), tile_prefix.T.reshape(-1).astype(jnp.int32), x1,
                        w["ln2_g"], w["ln2_b"], cap)


def _trunk(x, weights):
    B, S, _ = x.shape
    tabs = _rope_tables(S)
    x2 = x.reshape(B * S, D_MODEL)
    for l in range(DEPTH):
        x2 = _layer(x2, B, S, l, weights[l], tabs)
    return x2.reshape(B, S, D_MODEL)


def kernel(x_prompt, x_sample, w_in, diff_lambda, diff_subln_g, hgrn_lb_logits, hgrn_norm_g, mla_q_norm_g,
           mla_w_uq, mla_kv_norm_g, mla_w_ukv, rg_conv_w, rg_conv_b, rg_w_a, rg_b_a, rg_w_x, rg_b_x, rg_lambda,
           w_branch, w_out, ln_g, ln_b, w_router, w_e_gate, w_e_up, w_e_down):
    lb_all = jnp.cumsum(jax.nn.softmax(hgrn_lb_logits.astype(F32), axis=0), axis=0)
    lb_all = lb_all - lb_all[:1]
    p = dict(w_in=w_in, diff_lambda=diff_lambda, diff_subln_g=diff_subln_g, hgrn_norm_g=hgrn_norm_g,
             mla_q_norm_g=mla_q_norm_g, mla_w_uq=mla_w_uq, mla_kv_norm_g=mla_kv_norm_g, mla_w_ukv=mla_w_ukv,
             rg_conv_w=rg_conv_w, rg_conv_b=rg_conv_b, rg_w_a=rg_w_a, rg_b_a=rg_b_a, rg_w_x=rg_w_x,
             rg_b_x=rg_b_x, rg_lambda=rg_lambda, w_branch=w_branch, w_out=w_out, ln_g=ln_g, ln_b=ln_b,
             w_router=w_router, w_e_gate=w_e_gate, w_e_up=w_e_up, w_e_down=w_e_down)
    weights = [_prep_layer(l, p, lb_all) for l in range(DEPTH)]
    return (_trunk(x_prompt, weights), _trunk(x_sample, weights))
```

```python
import functools
import math

import numpy as np
import jax
import jax.numpy as jnp
from jax import lax
from jax.experimental import pallas as pl
from jax.experimental.pallas import tpu as pltpu

F32 = jnp.float32
BF16 = jnp.bfloat16

D_MODEL = 1024
DEPTH = 2
N_BRANCH = 4
BR_W = 256
A_HEADS = 4
A_DK = 32
A_DV = 64
B_HEADS = 4
B_DK = 64
C_HEADS = 4
C_NOPE = 32
C_ROPE = 16
C_QK = C_NOPE + C_ROPE
C_V = 64
C_Q_LORA = 192
C_KV_LORA = 128
C_PAD_W = 384
ROPE_BASE = 10000.0
D_BLOCKS = 4
CONV_W = 4
CONV_LEFT = 2
RG_C = 8.0
N_EXPERTS = 16
D_EXPERT = 2048
CAPACITY_FACTOR = 2
CHUNK = 128
LN_EPS = 1e-5
RMS_EPS = 1e-6
ALPHA = (2 * DEPTH) ** 0.25
ALIBI_SLOPES = tuple(2.0 ** (-8.0 * (h + 1) / A_HEADS) for h in range(A_HEADS))
A_NP = 2 * A_HEADS
A_QKW = 48
V_AUG_W = 128
ATT_TQ = 256
ATT_TK = 1024

VMEM_LIMIT = 56 * 1024 * 1024

NT_DIMS = (((1,), (1,)), ((), ()))
TN_DIMS = (((0,), (0,)), ((), ()))


def _cparams(sem):
    return pltpu.CompilerParams(dimension_semantics=sem, vmem_limit_bytes=VMEM_LIMIT)


def _dot(a, b):
    return jnp.dot(a, b, preferred_element_type=F32)


def _dot_nt(a, b):
    return lax.dot_general(a, b, NT_DIMS, preferred_element_type=F32)


def _dot_tn(a, b):
    return lax.dot_general(a, b, TN_DIMS, preferred_element_type=F32)


def _dot_split(a, b_bf16):
    hi = a.astype(BF16)
    lo = (a - hi.astype(F32)).astype(BF16)
    return _dot(hi, b_bf16) + _dot(lo, b_bf16)


def _sigmoid(x):
    return 1.0 / (1.0 + jnp.exp(-x))


def _pick(n, pref):
    t = min(pref, n)
    while n % t:
        t //= 2
    return t


def _in_proj_kernel(x_ref, waq, wakT, wav, qpos, kpos, vones, wb, wc, wd, aq_o, akT_o, av_o, b_o, c_o, d_o):
    xb = x_ref[...].astype(BF16)
    aq_o[...] = (_dot(xb, waq[...]) * (A_DK ** -0.5) + qpos[...]).astype(BF16)
    akT_o[0] = (_dot_nt(wakT[...], xb) + kpos[...]).astype(BF16)
    av_o[0] = (_dot(xb, wav[...]) + vones[...]).astype(BF16)
    b_o[...] = _dot(xb, wb[...])
    c_o[...] = _dot(xb, wc[...])
    d_o[...] = _dot(xb, wd[...])


def in_proj(x2, B, S, w, tabs):
    T = B * S
    tm = _pick(S, 512)
    ns = S // tm
    aw = A_NP * A_QKW
    vw = A_HEADS * V_AUG_W
    full = lambda shape: pl.BlockSpec(shape, lambda i: (0,) * len(shape))
    rows = lambda wdt: pl.BlockSpec((tm, wdt), lambda i: (i, 0))
    return pl.pallas_call(
        _in_proj_kernel,
        grid=(T // tm,),
        in_specs=[rows(D_MODEL), full((D_MODEL, aw)), full((aw, D_MODEL)), full((D_MODEL, vw)),
                  pl.BlockSpec((tm, aw), lambda i: (i % ns, 0)),
                  pl.BlockSpec((aw, tm), lambda i: (0, i % ns)),
                  full((1, vw)),
                  full((D_MODEL, 5 * BR_W)), full((D_MODEL, C_PAD_W)), full((D_MODEL, 2 * BR_W))],
        out_specs=[rows(aw),
                   pl.BlockSpec((1, aw, tm), lambda i: (i // ns, 0, i % ns)),
                   pl.BlockSpec((1, tm, vw), lambda i: (i // ns, i % ns, 0)),
                   rows(5 * BR_W), rows(C_PAD_W), rows(2 * BR_W)],
        out_shape=[jax.ShapeDtypeStruct((T, aw), BF16),
                   jax.ShapeDtypeStruct((B, aw, S), BF16),
                   jax.ShapeDtypeStruct((B, S, vw), BF16),
                   jax.ShapeDtypeStruct((T, 5 * BR_W), F32),
                   jax.ShapeDtypeStruct((T, C_PAD_W), F32),
                   jax.ShapeDtypeStruct((T, 2 * BR_W), F32)],
        compiler_params=_cparams(("parallel",)),
        name="in_proj",
    )(x2, w["aq"], w["akT"], w["av"], tabs["a_qpos"], tabs["a_kpos"], tabs["v_ones"], w["b"], w["c"], w["d"])


def _softmax_tile(n_prob, qk, v_of, c_of, corr_of, m_s, acc_s):
    s_next = qk(0)
    for p in range(n_prob):
        s = s_next
        if p + 1 < n_prob:
            s_next = qk(p + 1)
        corr = corr_of(p)
        if corr is not None:
            s = s + corr
        c = c_of(p)
        m_old = m_s[p]
        m_new = jnp.maximum(m_old, jnp.max(s, axis=1, keepdims=True) + c)
        alpha = jnp.exp(m_old - m_new)
        pr = jnp.exp(s - (m_new - c))
        acc_s[p] = alpha * acc_s[p] + _dot(pr.astype(BF16), v_of(p))
        m_s[p] = m_new


def _diff_attn_kernel(lam_ref, q_ref, kT_ref, v_ref, g_ref, o_ref, qs, m_s, acc_s, *, tq, tk, nk, lambda_init):
    qi = pl.program_id(1)
    lane = lax.broadcasted_iota(jnp.int32, (1, A_NP * A_QKW), 1)
    is_pos = (lane % A_QKW) >= A_DK
    qa = q_ref[...].astype(F32)
    qneg = jnp.where(is_pos, -qa, qa)
    for p in range(A_NP):
        qs[p] = qa[:, A_QKW * p:A_QKW * (p + 1)].astype(BF16)
        qs[A_NP + p] = qneg[:, A_QKW * p:A_QKW * (p + 1)].astype(BF16)
    m_s[...] = jnp.full(m_s.shape, -jnp.inf, F32)
    acc_s[...] = jnp.zeros(acc_s.shape, F32)

    def tile(j, variant, straddle):
        j0 = pl.multiple_of(j * tk, tk)
        off = (qi * tq - j0).astype(F32)
        if straddle:
            d = (lax.broadcasted_iota(jnp.int32, (tq, tk), 0)
                 - lax.broadcasted_iota(jnp.int32, (tq, tk), 1)).astype(F32) + off
            dpos = jnp.maximum(d, 0.0)

        def qk(p):
            return _dot(qs[variant * A_NP + p], kT_ref[0, A_QKW * p:A_QKW * (p + 1), pl.ds(j0, tk)])

        _softmax_tile(
            A_NP, qk,
            v_of=lambda p: v_ref[0, pl.ds(j0, tk), V_AUG_W * (p // 2):V_AUG_W * (p // 2 + 1)],
            c_of=lambda p: (off if variant == 0 else -off) * ALIBI_SLOPES[p // 2],
            corr_of=lambda p: dpos * (-2.0 * ALIBI_SLOPES[p // 2]) if straddle else None,
            m_s=m_s, acc_s=acc_s)

    def left(j, carry):
        tile(j, 1, False)
        return carry

    def right(j, carry):
        tile(j, 0, False)
        return carry

    jd = qi // (tk // tq)
    lax.fori_loop(0, jd, left, 0)
    tile(jd, 0, True)
    lax.fori_loop(jd + 1, nk, right, 0)

    lam = lam_ref[0]
    for h in range(A_HEADS):
        a0 = acc_s[2 * h]
        a1 = acc_s[2 * h + 1]
        o = a0[:, :A_DV] / a0[:, A_DV:A_DV + 1] - lam * (a1[:, :A_DV] / a1[:, A_DV:A_DV + 1])
        ms = jnp.mean(o * o, axis=1, keepdims=True)
        o = o * lax.rsqrt(ms + RMS_EPS) * g_ref[...] * (1.0 - lambda_init)
        o_ref[:, A_DV * h:A_DV * (h + 1)] = o.astype(BF16)


def _att_tiles(S):
    tk = _pick(S, ATT_TK)
    tq = _pick(tk, ATT_TQ)
    return tq, tk


def diff_attention(aq, akT, av, lam, subln_g, layer_idx, B, S):
    tq, tk = _att_tiles(S)
    nq = S // tq
    aw = A_NP * A_QKW
    vw = A_HEADS * V_AUG_W
    lambda_init = 0.8 - 0.6 * math.exp(-0.3 * layer_idx)
    kern = functools.partial(_diff_attn_kernel, tq=tq, tk=tk, nk=S // tk, lambda_init=lambda_init)
    return pl.pallas_call(
        kern,
        grid=(B, nq),
        in_specs=[pl.BlockSpec(memory_space=pltpu.SMEM),
                  pl.BlockSpec((tq, aw), lambda b, i: (b * nq + i, 0)),
                  pl.BlockSpec((1, aw, S), lambda b, i: (b, 0, 0)),
                  pl.BlockSpec((1, S, vw), lambda b, i: (b, 0, 0)),
                  pl.BlockSpec((1, A_DV), lambda b, i: (0, 0))],
        out_specs=pl.BlockSpec((tq, BR_W), lambda b, i: (b * nq + i, 0)),
        out_shape=jax.ShapeDtypeStruct((B * S, BR_W), BF16),
        scratch_shapes=[pltpu.VMEM((2 * A_NP, tq, A_QKW), BF16),
                        pltpu.VMEM((A_NP, tq, 1), F32),
                        pltpu.VMEM((A_NP, tq, V_AUG_W), F32)],
        compiler_params=_cparams(("parallel", "arbitrary")),
        name="diff_attn",
    )(lam, aq, akT, av, subln_g)


def _mla_prep_kernel(c_ref, qg_ref, kvg_ref, wuq, wuq_rot, wukT, wuv, vones, cosq, sinq, cosk, sink, eye_ref,
                     q_o, kT_o, v_o):
    c = c_ref[...]
    cq = c[:, :C_Q_LORA]
    ckv = c[:, C_Q_LORA:C_Q_LORA + C_KV_LORA]
    ckr = c[:, C_Q_LORA + C_KV_LORA:C_Q_LORA + C_KV_LORA + C_ROPE]
    ckr_rot = c[:, C_Q_LORA + C_KV_LORA + C_ROPE:C_Q_LORA + C_KV_LORA + 2 * C_ROPE]
    nq = (cq * lax.rsqrt(jnp.mean(cq * cq, axis=1, keepdims=True) + RMS_EPS) * qg_ref[...]).astype(BF16)
    q = _dot(nq, wuq[...]) * cosq[...] + _dot(nq, wuq_rot[...]) * sinq[...]
    q_o[...] = (q * (C_QK ** -0.5)).astype(BF16)
    nkv = (ckv * lax.rsqrt(jnp.mean(ckv * ckv, axis=1, keepdims=True) + RMS_EPS) * kvg_ref[...]).astype(BF16)
    knT = _dot_nt(wukT[...], nkv).astype(BF16)
    v_o[0] = (_dot(nkv, wuv[...]) + vones[...]).astype(BF16)
    kr = (ckr * cosk[...] + ckr_rot * sink[...]).astype(BF16)
    krT = _dot_nt(eye_ref[...], kr).astype(BF16)
    for h in range(C_HEADS):
        kT_o[0, C_QK * h:C_QK * h + C_NOPE, :] = knT[C_NOPE * h:C_NOPE * (h + 1), :]
        kT_o[0, C_QK * h + C_NOPE:C_QK * (h + 1), :] = krT


def mla_prep(c_all, w, tabs, B, S):
    T = B * S
    tm = _pick(S, 512)
    ns = S // tm
    full = lambda shape: pl.BlockSpec(shape, lambda i: (0,) * len(shape))
    rows = lambda wdt: pl.BlockSpec((tm, wdt), lambda i: (i, 0))
    pos = lambda wdt: pl.BlockSpec((tm, wdt), lambda i: (i % ns, 0))
    qw = C_HEADS * C_QK
    vw = C_HEADS * V_AUG_W
    return pl.pallas_call(
        _mla_prep_kernel,
        grid=(T // tm,),
        in_specs=[rows(C_PAD_W), full((1, C_Q_LORA)), full((1, C_KV_LORA)), full((C_Q_LORA, qw)),
                  full((C_Q_LORA, qw)), full((C_HEADS * C_NOPE, C_KV_LORA)), full((C_KV_LORA, vw)),
                  full((1, vw)), pos(qw), pos(qw), pos(C_ROPE), pos(C_ROPE), full((C_ROPE, C_ROPE))],
        out_specs=[rows(qw),
                   pl.BlockSpec((1, qw, tm), lambda i: (i // ns, 0, i % ns)),
                   pl.BlockSpec((1, tm, vw), lambda i: (i // ns, i % ns, 0))],
        out_shape=[jax.ShapeDtypeStruct((T, qw), BF16),
                   jax.ShapeDtypeStruct((B, qw, S), BF16),
                   jax.ShapeDtypeStruct((B, S, vw), BF16)],
        compiler_params=_cparams(("parallel",)),
        name="mla_prep",
    )(c_all, w["q_norm_g"], w["kv_norm_g"], w["wuq"], w["wuq_rot"], w["wukT"], w["wuv"], tabs["v_ones"],
      tabs["cosq"], tabs["sinq"], tabs["cosk"], tabs["sink"], tabs["eye"])


def _mla_attn_kernel(q_ref, kT_ref, v_ref, o_ref, qs, m_s, acc_s, *, tk, nk):
    for h in range(C_HEADS):
        qs[h] = q_ref[:, C_QK * h:C_QK * (h + 1)]
    m_s[...] = jnp.full(m_s.shape, -jnp.inf, F32)
    acc_s[...] = jnp.zeros(acc_s.shape, F32)

    def tile(j, carry):
        j0 = pl.multiple_of(j * tk, tk)
        _softmax_tile(
            C_HEADS,
            qk=lambda h: _dot(qs[h], kT_ref[0, C_QK * h:C_QK * (h + 1), pl.ds(j0, tk)]),
            v_of=lambda h: v_ref[0, pl.ds(j0, tk), V_AUG_W * h:V_AUG_W * (h + 1)],
            c_of=lambda h: 0.0,
            corr_of=lambda h: None,
            m_s=m_s, acc_s=acc_s)
        return carry

    lax.fori_loop(0, nk, tile, 0)
    for h in range(C_HEADS):
        a = acc_s[h]
        o_ref[:, C_V * h:C_V * (h + 1)] = (a[:, :C_V] / a[:, C_V:C_V + 1]).astype(BF16)


def mla_attention(q, kT, v, B, S):
    tq, tk = _att_tiles(S)
    nq = S // tq
    qw = C_HEADS * C_QK
    vw = C_HEADS * V_AUG_W
    kern = functools.partial(_mla_attn_kernel, tk=tk, nk=S // tk)
    return pl.pallas_call(
        kern,
        grid=(B, nq),
        in_specs=[pl.BlockSpec((tq, qw), lambda b, i: (b * nq + i, 0)),
                  pl.BlockSpec((1, qw, S), lambda b, i: (b, 0, 0)),
                  pl.BlockSpec((1, S, vw), lambda b, i: (b, 0, 0))],
        out_specs=pl.BlockSpec((tq, BR_W), lambda b, i: (b * nq + i, 0)),
        out_shape=jax.ShapeDtypeStruct((B * S, BR_W), BF16),
        scratch_shapes=[pltpu.VMEM((C_HEADS, tq, C_QK), BF16),
                        pltpu.VMEM((C_HEADS, tq, 1), F32),
                        pltpu.VMEM((C_HEADS, tq, V_AUG_W), F32)],
        compiler_params=_cparams(("parallel", "arbitrary")),
        name="mla_attn",
    )(q, kT, v)


HGRN_LEVELS = (64, 32, 16, 8, 4, 2, 1)
N_HGRN_SUMS = len(HGRN_LEVELS) + 2


def _hgrn_constants():
    C = CHUNK
    t = np.arange(C)
    sums = np.zeros((2, N_HGRN_SUMS, C, C), np.float32)
    masks = np.zeros((2, len(HGRN_LEVELS) + 1, C, C), np.float32)
    for li, m in enumerate(HGRN_LEVELS):
        blk = t // m
        for tt in range(C):
            if blk[tt] % 2 == 1:
                sums[0, li, tt, blk[tt] * m:tt + 1] = 1.0
            else:
                sums[0, li, tt, tt + 1:(blk[tt] + 1) * m] = 1.0
        masks[0, li] = ((blk[:, None] % 2 == 1) & (blk[None, :] == blk[:, None] - 1)).astype(np.float32)
    masks[0, len(HGRN_LEVELS)] = np.eye(C, dtype=np.float32)
    sums[0, len(HGRN_LEVELS)] = (t[None, :] <= t[:, None]).astype(np.float32)
    sums[0, len(HGRN_LEVELS) + 1] = (t[None, :] > t[:, None]).astype(np.float32)
    sums[1] = sums[0][:, ::-1, ::-1]
    masks[1] = masks[0][:, ::-1, ::-1]
    return sums.reshape(2, N_HGRN_SUMS * C, C), np.concatenate([masks, masks], axis=3)


def _hgrn_kernel(qf_ref, ff_ref, vf_ref, qb_ref, fb_ref, vb_ref, lb_ref, sums_ref, mask_ref,
                 of_ref, ob_ref, st_ref):
    C = CHUNK
    nlev = len(HGRN_LEVELS)

    @pl.when(pl.program_id(1) == 0)
    def _():
        st_ref[...] = jnp.zeros(st_ref.shape, F32)

    lane = lax.broadcasted_iota(jnp.int32, (1, 2 * B_DK), 1)
    head_lanes = (lane < B_DK, lane >= B_DK)
    r_i = lax.broadcasted_iota(jnp.int32, (2 * B_DK, 2 * B_DK), 0)
    c_i = lax.broadcasted_iota(jnp.int32, (2 * B_DK, 2 * B_DK), 1)
    same_head = (r_i < B_DK) == (c_i < B_DK)

    for d, (q_ref, f_ref, v_ref, o_ref) in enumerate(((qf_ref, ff_ref, vf_ref, of_ref),
                                                      (qb_ref, fb_ref, vb_ref, ob_ref))):
        x = q_ref[...]
        q = x * _sigmoid(x)
        lb = lb_ref[d]
        f = lb + (1.0 - lb) * _sigmoid(f_ref[...])
        logf = jnp.log(f)
        k = 1.0 - f
        v = v_ref[...]
        hi = logf.astype(BF16)
        lo = (logf - hi.astype(F32)).astype(BF16)
        sums = sums_ref[d]
        E = _dot(sums, hi) + _dot(sums, lo)
        for p in range(B_HEADS // 2):
            sl = slice(2 * B_DK * p, 2 * B_DK * (p + 1))
            Qp, Kp, Vp = q[:, sl], k[:, sl], v[:, sl]

            def by_head(a):
                return jnp.concatenate([jnp.where(head_lanes[0], a, 0.0), jnp.where(head_lanes[1], a, 0.0)],
                                       axis=0).astype(BF16)

            A = jnp.zeros((C, 2 * C), F32)
            for l in range(nlev + 1):
                if l < nlev:
                    G = jnp.exp(E[C * l:C * (l + 1), sl])
                    QG = (Qp * G).astype(BF16)
                    KG = Kp * G
                else:
                    QG = Qp.astype(BF16)
                    KG = Kp
                A = A + mask_ref[d, l] * _dot_nt(QG, by_head(KG))
            o = _dot(A.astype(BF16), by_head(Vp))
            eb = jnp.exp(E[C * nlev:C * (nlev + 1), sl])
            st = st_ref[d, p]
            o = o + _dot_nt((Qp * eb).astype(BF16), st.astype(BF16))
            o_ref[:, sl] = o
            Kd = (Kp * jnp.exp(E[C * (nlev + 1):C * (nlev + 2), sl])).astype(BF16)
            row = C - 1 if d == 0 else 0
            g = eb[row:row + 1, :]
            upd = _dot_tn(Vp.astype(BF16), Kd)
            st_ref[d, p] = st * g + jnp.where(same_head, upd, 0.0)


def hgrn2(b_all, lb, B, S):
    T = B * S
    C = CHUNK
    nc = S // C
    sums, masks = _hgrn_constants()
    sums = jnp.asarray(sums, BF16)
    masks = jnp.asarray(masks, F32)
    fwd = lambda col: pl.BlockSpec((C, BR_W), lambda b, n: (b * nc + n, col))
    bwd = lambda col: pl.BlockSpec((C, BR_W), lambda b, n: (b * nc + nc - 1 - n, col))
    full = lambda shape: pl.BlockSpec(shape, lambda b, n: (0,) * len(shape))
    return pl.pallas_call(
        _hgrn_kernel,
        grid=(B, nc),
        in_specs=[fwd(0), fwd(1), fwd(3), bwd(0), bwd(2), bwd(3),
                  full((2, 1, BR_W)), full(sums.shape), full(masks.shape)],
        out_specs=[pl.BlockSpec((C, BR_W), lambda b, n: (b * nc + n, 0)),
                   pl.BlockSpec((C, BR_W), lambda b, n: (b * nc + nc - 1 - n, 0))],
        out_shape=[jax.ShapeDtypeStruct((T, BR_W), F32), jax.ShapeDtypeStruct((T, BR_W), F32)],
        scratch_shapes=[pltpu.VMEM((2, B_HEADS // 2, 2 * B_DK, 2 * B_DK), F32)],
        compiler_params=_cparams(("parallel", "arbitrary")),
        name="hgrn2",
    )(b_all, b_all, b_all, b_all, b_all, b_all, lb, sums, masks)


HALO = 8


def _rglru_kernel(xf_ref, xfp_ref, xfn_ref, xb_ref, xbp_ref, xbn_ref, cw_ref, cb_ref, wa_ref, ba_ref,
                  wx_ref, bx_ref, sp_ref, hf_ref, hb_ref, carry_ref, *, tc):
    n = pl.program_id(1)
    nc = pl.num_programs(1)

    @pl.when(n == 0)
    def _():
        carry_ref[...] = jnp.zeros(carry_ref.shape, F32)

    row = lax.broadcasted_iota(jnp.int32, (tc, 1), 0)
    for d, (x_ref, xp_ref, xn_ref, o_ref) in enumerate(((xf_ref, xfp_ref, xfn_ref, hf_ref),
                                                        (xb_ref, xbp_ref, xbn_ref, hb_ref))):
        cidx = n if d == 0 else nc - 1 - n
        prev = jnp.where(cidx > 0, xp_ref[...], 0.0)
        nxt = jnp.where(cidx < nc - 1, xn_ref[...], 0.0)
        ext = jnp.concatenate([prev, x_ref[...], nxt], axis=0)
        xc = cb_ref[...]
        for j in range(CONV_W):
            off = HALO - CONV_LEFT + j
            xc = xc + ext[off:off + tc, :] * cw_ref[j:j + 1, :]
        xcb = xc.astype(BF16)
        r = _sigmoid(_dot(xcb, wa_ref[d]) + ba_ref[d])
        i = _sigmoid(_dot(xcb, wx_ref[d]) + bx_ref[d])
        log_a = (-RG_C) * r * sp_ref[d]
        a = jnp.exp(log_a)
        u = jnp.sqrt(1.0 - jnp.exp(2.0 * log_a)) * (i * xc)
        sh = 1
        while sh < tc:
            if d == 0:
                valid = row >= sh
                amt = sh
            else:
                valid = row < tc - sh
                amt = tc - sh
            a_sh = jnp.where(valid, pltpu.roll(a, amt, 0), 1.0)
            u_sh = jnp.where(valid, pltpu.roll(u, amt, 0), 0.0)
            u = a * u_sh + u
            a = a * a_sh
            sh *= 2
        h = u + a * carry_ref[d]
        o_ref[...] = h
        last = tc - 1 if d == 0 else 0
        carry_ref[d] = h[last:last + 1, :]


def rglru(d_all, w, B, S):
    T = B * S
    tc = _pick(S, 512)
    nc = S // tc
    hb = tc // HALO
    nrow8 = T // HALO
    fc = lambda b, n: b * nc + n
    bc = lambda b, n: b * nc + nc - 1 - n
    cur = lambda f: pl.BlockSpec((tc, BR_W), lambda b, n: (f(b, n), 0))
    prv = lambda f: pl.BlockSpec((HALO, BR_W), lambda b, n: (jnp.maximum(f(b, n) * hb - 1, 0), 0))
    nxt = lambda f: pl.BlockSpec((HALO, BR_W), lambda b, n: (jnp.minimum((f(b, n) + 1) * hb, nrow8 - 1), 0))
    full = lambda shape: pl.BlockSpec(shape, lambda b, n: (0,) * len(shape))
    kern = functools.partial(_rglru_kernel, tc=tc)
    return pl.pallas_call(
        kern,
        grid=(B, nc),
        in_specs=[cur(fc), prv(fc), nxt(fc), cur(bc), prv(bc), nxt(bc),
                  full((CONV_W, BR_W)), full((1, BR_W)), full((2, BR_W, BR_W)), full((2, 1, BR_W)),
                  full((2, BR_W, BR_W)), full((2, 1, BR_W)), full((2, 1, BR_W))],
        out_specs=[cur(fc), cur(bc)],
        out_shape=[jax.ShapeDtypeStruct((T, BR_W), F32), jax.ShapeDtypeStruct((T, BR_W), F32)],
        scratch_shapes=[pltpu.VMEM((2, 1, BR_W), F32)],
        compiler_params=_cparams(("parallel", "arbitrary")),
        name="rglru",
    )(d_all, d_all, d_all, d_all, d_all, d_all, w["conv_w"], w["conv_b"], w["wa"], w["ba"], w["wx"],
      w["bx"], w["sp"])


def _layer_norm(z, g, b):
    mu = jnp.mean(z, axis=1, keepdims=True)
    zc = z - mu
    var = jnp.mean(zc * zc, axis=1, keepdims=True)
    return zc * lax.rsqrt(var + LN_EPS) * g + b


def _merge_kernel(x_ref, ya_ref, of_ref, ob_ref, bg_ref, yc_ref, hf_ref, hb_ref, dg_ref, wg_ref, wbr_ref,
                  wo_ref, lng_ref, lnb_ref, hg_ref, havg_ref, wr_ref, x1_ref, aff_ref):
    x = x_ref[...]
    xb = x.astype(BF16)
    o = of_ref[...] + ob_ref[...]
    ms = _dot_split(o * o, havg_ref[...])
    bg = bg_ref[...]
    yb = o * lax.rsqrt(ms + RMS_EPS) * hg_ref[...] * (bg * _sigmoid(bg))
    dg = dg_ref[...]
    gelu = 0.5 * dg * (1.0 + jnp.tanh(0.7978845608028654 * (dg + 0.044715 * dg * dg * dg)))
    yd = (hf_ref[...] + hb_ref[...]) * gelu
    ys = (ya_ref[...], yb.astype(BF16), yc_ref[...], yd.astype(BF16))
    mix = None
    for i in range(N_BRANCH):
        gate = _sigmoid(_dot(xb, wg_ref[:, D_MODEL * i:D_MODEL * (i + 1)]))
        term = gate * _dot(ys[i], wbr_ref[i])
        mix = term if mix is None else mix + term
    z = ALPHA * x + _dot(mix.astype(BF16), wo_ref[...])
    x1 = _layer_norm(z, lng_ref[...], lnb_ref[...])
    x1_ref[...] = x1
    x1h = x1.astype(BF16)
    x1l = (x1 - x1h.astype(F32)).astype(BF16)
    logits = _dot_nt(x1h, wr_ref[0]) + (_dot_nt(x1l, wr_ref[0]) + _dot_nt(x1h, wr_ref[1]))
    e = jnp.exp(logits - jnp.max(logits, axis=1, keepdims=True))
    aff_ref[...] = e / jnp.sum(e, axis=1, keepdims=True)


def merge(x2, ya, of, ob, b_all, yc, hf, hb, d_all, w):
    T = x2.shape[0]
    tm = _pick(T, 512)
    full = lambda shape: pl.BlockSpec(shape, lambda i: (0,) * len(shape))
    rows = lambda wdt: pl.BlockSpec((tm, wdt), lambda i: (i, 0))
    colblk = lambda c: pl.BlockSpec((tm, BR_W), lambda i: (i, c))
    return pl.pallas_call(
        _merge_kernel,
        grid=(T // tm,),
        in_specs=[rows(D_MODEL), rows(BR_W), rows(BR_W), rows(BR_W), colblk(4), rows(BR_W), rows(BR_W),
                  rows(BR_W), colblk(1), full((D_MODEL, N_BRANCH * D_MODEL)),
                  full((N_BRANCH, BR_W, D_MODEL)), full((D_MODEL, D_MODEL)), full((1, D_MODEL)),
                  full((1, D_MODEL)), full((1, BR_W)), full((BR_W, BR_W)), full((2, N_EXPERTS, D_MODEL))],
        out_specs=[rows(D_MODEL), rows(N_EXPERTS)],
        out_shape=[jax.ShapeDtypeStruct((T, D_MODEL), F32), jax.ShapeDtypeStruct((T, N_EXPERTS), F32)],
        compiler_params=_cparams(("parallel",)),
        name="merge",
    )(x2, ya, of, ob, b_all, yc, hf, hb, d_all, w["gate"], w["branch"], w["out"], w["ln_g"], w["ln_b"],
      w["hgrn_norm_g"], w["head_avg"], w["router"])


def _select_kernel(aff_ref, sel_ref, *, cap, n_tok):
    aff = aff_ref[...]
    v = lax.bitcast_convert_type(aff, jnp.int32)
    E, R, L = aff.shape

    def count(pred):
        c = jnp.sum(pred.astype(F32), axis=1)
        return jnp.sum(c, axis=1, keepdims=True)

    def value_step(i, thr):
        cand = thr | jnp.left_shift(jnp.int32(1), 30 - i)
        ok = count(v >= cand[:, :, None]) >= cap
        return jnp.where(ok, cand, thr)

    thr = lax.fori_loop(0, 31, value_step, jnp.zeros((E, 1), jnp.int32))
    thr3 = thr[:, :, None]
    gt = v > thr3
    eq = v == thr3
    need = cap - count(gt)
    tok = (lax.broadcasted_iota(jnp.int32, (E, R, L), 1) * L
           + lax.broadcasted_iota(jnp.int32, (E, R, L), 2))
    nbits = max(1, int(n_tok).bit_length())

    def index_step(i, bound):
        cand = bound | jnp.left_shift(jnp.int32(1), nbits - 1 - i)
        ok = count(eq & (tok < cand[:, :, None])) <= need
        return jnp.where(ok, cand, bound)

    bound = lax.fori_loop(0, nbits, index_step, jnp.zeros((E, 1), jnp.int32))
    sel = gt | (eq & (tok < bound[:, :, None]))
    sel_ref[...] = sel.astype(jnp.int32)


def select_tokens(affT3, cap, n_tok):
    E, R, L = affT3.shape
    kern = functools.partial(_select_kernel, cap=cap, n_tok=n_tok)
    return pl.pallas_call(
        kern,
        grid=(1,),
        in_specs=[pl.BlockSpec((E, R, L), lambda i: (0, 0, 0))],
        out_specs=pl.BlockSpec((E, R, L), lambda i: (0, 0, 0)),
        out_shape=jax.ShapeDtypeStruct((E, R, L), jnp.int32),
        compiler_params=_cparams(("arbitrary",)),
        name="select_tokens",
    )(affT3)


FFN_FCHUNK = 512


def _ffn_kernel(idx_hbm, x_hbm, g_ref, wg_ref, wu_ref, wd_ref, ye_ref, idx_smem, xbuf, idx_sem, row_sem,
                *, tm, n_blocks):
    pair = pl.program_id(0) * pl.num_programs(1) + pl.program_id(1)
    last = n_blocks - 1

    def idx_copy(b, s):
        return pltpu.make_async_copy(idx_hbm.at[b], idx_smem.at[s], idx_sem.at[s])

    def row_copy(src_row, s, j):
        return pltpu.make_async_copy(x_hbm.at[pl.ds(src_row, 1), :], xbuf.at[s, pl.ds(j, 1), :],
                                     row_sem.at[s])

    def wait_rows(s):
        pltpu.make_async_copy(x_hbm.at[pl.ds(0, tm), :], xbuf.at[s], row_sem.at[s]).wait()

    @pl.when(pair == 0)
    def _():
        first = idx_copy(0, 0)
        first.start()
        first.wait()

        def issue(j, carry):
            row_copy(idx_smem[0, j], 0, j).start()
            return carry

        lax.fori_loop(0, tm, issue, 0)
        idx_copy(jnp.minimum(1, last), 1).start()

    n_chunks = D_EXPERT // FFN_FCHUNK
    per_chunk = tm // n_chunks
    for slot in range(2):
        nslot = 1 - slot
        blk = 2 * pair + slot
        idx_copy(0, nslot).wait()
        wait_rows(slot)
        xb = xbuf[slot].astype(BF16)
        acc = jnp.zeros((tm, D_MODEL), F32)
        for c in range(n_chunks):
            for j in range(per_chunk * c, per_chunk * (c + 1)):
                row_copy(idx_smem[nslot, j], nslot, j).start()
            fs = slice(FFN_FCHUNK * c, FFN_FCHUNK * (c + 1))
            hg = _dot(xb, wg_ref[0, :, fs])
            hu = _dot(xb, wu_ref[0, :, fs])
            h = (hg * _sigmoid(hg) * hu).astype(BF16)
            acc = acc + _dot(h, wd_ref[0, fs, :])
        ye_ref[tm * slot:tm * (slot + 1), :] = acc * g_ref[tm * slot:tm * (slot + 1), :]
        idx_copy(jnp.minimum(blk + 2, last), slot).start()

    @pl.when(2 * pair + 1 == last)
    def _():
        wait_rows(0)
        idx_copy(0, 1).wait()


def expert_ffn(idx, x1, g, wg, wu, wd):
    E, cap = idx.shape
    tm = _pick(cap // 2, 512)
    R = cap // tm
    RP = R // 2
    kern = functools.partial(_ffn_kernel, tm=tm, n_blocks=E * R)
    return pl.pallas_call(
        kern,
        grid=(E, RP),
        in_specs=[pl.BlockSpec(memory_space=pl.ANY),
                  pl.BlockSpec(memory_space=pl.ANY),
                  pl.BlockSpec((2 * tm, 1), lambda e, r: (e * RP + r, 0)),
                  pl.BlockSpec((1, D_MODEL, D_EXPERT), lambda e, r: (e, 0, 0)),
                  pl.BlockSpec((1, D_MODEL, D_EXPERT), lambda e, r: (e, 0, 0)),
                  pl.BlockSpec((1, D_EXPERT, D_MODEL), lambda e, r: (e, 0, 0))],
        out_specs=pl.BlockSpec((2 * tm, D_MODEL), lambda e, r: (e * RP + r, 0)),
        out_shape=jax.ShapeDtypeStruct((E * cap, D_MODEL), F32),
        scratch_shapes=[pltpu.SMEM((2, tm), jnp.int32),
                        pltpu.VMEM((2, tm, D_MODEL), F32),
                        pltpu.SemaphoreType.DMA((2,)),
                        pltpu.SemaphoreType.DMA((2,))],
        compiler_params=pltpu.CompilerParams(dimension_semantics=("arbitrary", "arbitrary"),
                                             vmem_limit_bytes=VMEM_LIMIT, disable_bounds_checks=True),
        name="expert_ffn",
    )(idx.reshape(E * R, tm), x1, g.reshape(E * cap, 1), wg, wu, wd)


CMB_TT = 128
CMB_PIECE = 32
CMB_CHUNK = 8
SUBLANES = 8


def _combine_kernel(p_ref, ye_hbm, sel_ref, x_ref, tri_ref, g_ref, b_ref, o_ref,
                    buf, meta, nfill, pre_ref, oh_ref, acc_ref, sem, *, cap, n_rows, n_tiles, max_pieces):
    i = pl.program_id(0)
    slot = i % 2
    E = N_EXPERTS
    PC = CMB_PIECE

    def piece_copy(src_row, s, k):
        return pltpu.make_async_copy(ye_hbm.at[pl.ds(src_row, PC), :], buf.at[s, pl.ds(k * PC, PC), :],
                                     sem.at[s])

    def issue_tile(tile, s):
        def expert_body(e, fill):
            lo = p_ref[tile * E + e]
            s0 = e * cap + lo
            s1 = e * cap + p_ref[(tile + 1) * E + e]
            a0 = (s0 // SUBLANES) * SUBLANES

            def piece(q, fill):
                start = a0 + PC * q
                start_c = pl.multiple_of(jnp.minimum(start, n_rows - PC), SUBLANES)
                piece_copy(start_c, s, fill).start()
                meta[s, 0, fill] = e
                meta[s, 1, fill] = start_c - s0
                meta[s, 2, fill] = start - start_c
                return fill + 1

            n_pieces = jnp.where(s1 > s0, (s1 - a0 + PC - 1) // PC, 0)
            return lax.fori_loop(0, n_pieces, piece, fill)

        nfill[s] = lax.fori_loop(0, E, expert_body, 0)

    @pl.when(i == 0)
    def _():
        buf[...] = jnp.zeros(buf.shape, F32)

        def clear(k, carry):
            for s in range(2):
                for f in range(3):
                    meta[s, f, k] = 0
            return carry

        lax.fori_loop(0, max_pieces, clear, 0)
        issue_tile(0, 0)

    @pl.when(i + 1 < n_tiles)
    def _():
        issue_tile(i + 1, 1 - slot)

    fill = nfill[slot]

    def wait_piece(k, carry):
        piece_copy(0, slot, k).wait()
        return carry

    lax.fori_loop(0, fill, wait_piece, 0)

    selv = sel_ref[...]
    pre_ref[...] = _dot(selv.astype(BF16), tri_ref[...])
    acc_ref[...] = jnp.zeros(acc_ref.shape, F32)
    row = lax.broadcasted_iota(jnp.int32, (PC, 1), 0)

    def chunk(c, carry):
        for k in range(CMB_CHUNK):
            pidx = c * CMB_CHUNK + k
            live = pidx < fill
            e = jnp.where(live, meta[slot, 0, pidx], 0)
            rank = jnp.where((row >= meta[slot, 2, pidx]) & live, row + meta[slot, 1, pidx], -1).astype(F32)
            hit = (pre_ref[pl.ds(e, 1), :] == rank) & (sel_ref[pl.ds(e, 1), :] > 0.5)
            oh_ref[PC * k:PC * (k + 1), :] = jnp.where(hit, 1.0, 0.0).astype(BF16)
        rows = buf[slot, pl.ds(pl.multiple_of(c * (CMB_CHUNK * PC), CMB_CHUNK * PC), CMB_CHUNK * PC), :]
        hi = rows.astype(BF16)
        lo = (rows - hi.astype(F32)).astype(BF16)
        oh = oh_ref[...]
        acc_ref[...] += _dot_tn(oh, hi) + _dot_tn(oh, lo)
        return carry

    lax.fori_loop(0, (fill + CMB_CHUNK - 1) // CMB_CHUNK, chunk, 0)
    o_ref[...] = _layer_norm(ALPHA * x_ref[...] + acc_ref[...], g_ref[...], b_ref[...])


def combine_norm(ye, sel_f, tile_prefix, x1, g, b, cap):
    T = x1.shape[0]
    tt = _pick(T, CMB_TT)
    n_tiles = T // tt
    max_pieces = N_EXPERTS * (tt // CMB_PIECE + 1)
    max_pieces = ((max_pieces + CMB_CHUNK - 1) // CMB_CHUNK) * CMB_CHUNK
    tri = jnp.asarray(np.triu(np.ones((tt, tt), np.float32), 1), BF16)
    kern = functools.partial(_combine_kernel, cap=cap, n_rows=ye.shape[0], n_tiles=n_tiles,
                             max_pieces=max_pieces)
    grid_spec = pltpu.PrefetchScalarGridSpec(
        num_scalar_prefetch=1,
        grid=(n_tiles,),
        in_specs=[pl.BlockSpec(memory_space=pl.ANY),
                  pl.BlockSpec((N_EXPERTS, tt), lambda i, p: (0, i)),
                  pl.BlockSpec((tt, D_MODEL), lambda i, p: (i, 0)),
                  pl.BlockSpec((tt, tt), lambda i, p: (0, 0)),
                  pl.BlockSpec((1, D_MODEL), lambda i, p: (0, 0)),
                  pl.BlockSpec((1, D_MODEL), lambda i, p: (0, 0))],
        out_specs=pl.BlockSpec((tt, D_MODEL), lambda i, p: (i, 0)),
        scratch_shapes=[pltpu.VMEM((2, max_pieces * CMB_PIECE, D_MODEL), F32),
                        pltpu.SMEM((2, 3, max_pieces), jnp.int32),
                        pltpu.SMEM((2,), jnp.int32),
                        pltpu.VMEM((N_EXPERTS, tt), F32),
                        pltpu.VMEM((CMB_CHUNK * CMB_PIECE, tt), BF16),
                        pltpu.VMEM((tt, D_MODEL), F32),
                        pltpu.SemaphoreType.DMA((2,))])
    return pl.pallas_call(
        kern,
        grid_spec=grid_spec,
        out_shape=jax.ShapeDtypeStruct((T, D_MODEL), F32),
        compiler_params=pltpu.CompilerParams(dimension_semantics=("arbitrary",), vmem_limit_bytes=VMEM_LIMIT,
                                             disable_bounds_checks=True),
        name="combine_norm",
    )(tile_prefix, ye, sel_f, x1, tri, g, b)


def _rot_cols(w, start):
    half = C_ROPE // 2
    return jnp.concatenate([-w[:, start + half:start + C_ROPE], w[:, start:start + half]], axis=1)


def _block_diag(w):
    n, c, _ = w.shape
    out = jnp.zeros((n * c, n * c), w.dtype)
    for i in range(n):
        out = out.at[i * c:(i + 1) * c, i * c:(i + 1) * c].set(w[i])
    return out


def _prep_layer(l, p, lb_all):
    w_in = p["w_in"][l]
    o = 0
    aq = w_in[:, o:o + BR_W]; o += BR_W
    ak = w_in[:, o:o + BR_W]; o += BR_W
    av = w_in[:, o:o + BR_W]; o += BR_W
    wb = w_in[:, o:o + 5 * BR_W]; o += 5 * BR_W
    c0 = o
    wcq = w_in[:, o:o + C_Q_LORA]; o += C_Q_LORA
    wckv = w_in[:, o:o + C_KV_LORA]; o += C_KV_LORA
    wckr = w_in[:, o:o + C_ROPE]; o += C_ROPE
    wd = w_in[:, o:o + 2 * BR_W]; o += 2 * BR_W
    wgate = w_in[:, o:o + N_BRANCH * D_MODEL]
    used = C_Q_LORA + C_KV_LORA + 2 * C_ROPE
    wc = jnp.concatenate([wcq, wckv, wckr, _rot_cols(w_in, c0 + C_Q_LORA + C_KV_LORA),
                          jnp.zeros((D_MODEL, C_PAD_W - used), F32)], axis=1)
    padq = jnp.zeros((D_MODEL, A_QKW - A_DK), F32)
    padv = jnp.zeros((D_MODEL, V_AUG_W - A_DV), F32)
    aq_w = jnp.concatenate([t for p_ in range(A_NP) for t in (aq[:, A_DK * p_:A_DK * (p_ + 1)], padq)], axis=1)
    ak_w = jnp.concatenate([t for p_ in range(A_NP) for t in (ak[:, A_DK * p_:A_DK * (p_ + 1)], padq)], axis=1)
    av_w = jnp.concatenate([t for h in range(A_HEADS) for t in (av[:, A_DV * h:A_DV * (h + 1)], padv)], axis=1)
    proj = dict(aq=aq_w.astype(BF16), akT=ak_w.T.astype(BF16), av=av_w.astype(BF16), b=wb.astype(BF16),
                c=wc.astype(BF16), d=wd.astype(BF16))

    wuq = p["mla_w_uq"][l]
    wuq_rot = jnp.zeros_like(wuq)
    for h in range(C_HEADS):
        r0 = C_QK * h + C_NOPE
        wuq_rot = wuq_rot.at[:, r0:r0 + C_ROPE].set(_rot_cols(wuq, r0))
    wukv = p["mla_w_ukv"][l].reshape(C_KV_LORA, C_HEADS, C_NOPE + C_V)
    wuv = jnp.concatenate([wukv[:, :, C_NOPE:], jnp.zeros((C_KV_LORA, C_HEADS, V_AUG_W - C_V), F32)], axis=2)
    mla = dict(q_norm_g=p["mla_q_norm_g"][l][None, :], kv_norm_g=p["mla_kv_norm_g"][l][None, :],
               wuq=wuq.astype(BF16), wuq_rot=wuq_rot.astype(BF16),
               wukT=wukv[:, :, :C_NOPE].reshape(C_KV_LORA, C_HEADS * C_NOPE).T.astype(BF16),
               wuv=wuv.reshape(C_KV_LORA, C_HEADS * V_AUG_W).astype(BF16))

    rg = dict(conv_w=p["rg_conv_w"][l], conv_b=p["rg_conv_b"][l][None, :],
              wa=jnp.stack([_block_diag(p["rg_w_a"][l, d]) for d in range(2)]).astype(BF16),
              ba=p["rg_b_a"][l][:, None, :],
              wx=jnp.stack([_block_diag(p["rg_w_x"][l, d]) for d in range(2)]).astype(BF16),
              bx=p["rg_b_x"][l][:, None, :],
              sp=jax.nn.softplus(-p["rg_lambda"][l].astype(F32))[:, None, :])

    head = np.arange(BR_W) // B_DK
    wr_hi = p["w_router"][l].astype(BF16)
    mrg = dict(gate=wgate.astype(BF16), branch=p["w_branch"][l].astype(BF16), out=p["w_out"][l].astype(BF16),
               ln_g=p["ln_g"][l, 0][None, :], ln_b=p["ln_b"][l, 0][None, :],
               hgrn_norm_g=p["hgrn_norm_g"][l][None, :],
               head_avg=jnp.asarray((head[:, None] == head[None, :]).astype(np.float32) / B_DK, BF16),
               router=jnp.stack([wr_hi.T, (p["w_router"][l] - wr_hi.astype(F32)).astype(BF16).T]))

    lp = p["diff_lambda"][l].astype(F32)
    lambda_init = 0.8 - 0.6 * math.exp(-0.3 * l)
    lam = (jnp.exp(jnp.sum(lp[0] * lp[1])) - jnp.exp(jnp.sum(lp[2] * lp[3])) + lambda_init).reshape(1)
    return dict(proj=proj, mla=mla, rg=rg, mrg=mrg, lam=lam, subln_g=p["diff_subln_g"][l][None, :],
                lb=lb_all[l][:, None, :],
                ffn=(p["w_e_gate"][l].astype(BF16), p["w_e_up"][l].astype(BF16), p["w_e_down"][l].astype(BF16)),
                ln2_g=p["ln_g"][l, 1][None, :], ln2_b=p["ln_b"][l, 1][None, :])


def _rope_tables(S):
    half = C_ROPE // 2
    inv = ROPE_BASE ** (-jnp.arange(0, C_ROPE, 2, dtype=F32) / C_ROPE)
    ang = jnp.arange(S, dtype=F32)[:, None] * inv[None, :]
    cos, sin = jnp.cos(ang), jnp.sin(ang)
    cosk = jnp.concatenate([cos, cos], axis=1)
    sink = jnp.concatenate([sin, sin], axis=1)
    ones = jnp.ones((S, C_NOPE), F32)
    zeros = jnp.zeros((S, C_NOPE), F32)
    cosq = jnp.concatenate([jnp.concatenate([ones, cosk], axis=1)] * C_HEADS, axis=1)
    sinq = jnp.concatenate([jnp.concatenate([zeros, sink], axis=1)] * C_HEADS, axis=1)
    tq, tk = _att_tiles(S)
    pos = jnp.arange(S, dtype=jnp.int32)
    i_rel = (pos % tq).astype(F32)
    j_rel = pos % tk
    j_hi = (4 * (j_rel // 4)).astype(F32)
    j_lo = (j_rel % 4).astype(F32)
    one = jnp.ones((S,), F32)
    qcols = jnp.stack([i_rel, one, one], axis=1)
    qpad = jnp.zeros((S, A_QKW - A_DK - 3), F32)
    qblock = jnp.concatenate([jnp.zeros((S, A_DK), F32), qcols, qpad], axis=1)
    a_qpos = jnp.concatenate([qblock] * A_NP, axis=1)
    kblocks = []
    for p_ in range(A_NP):
        sl = ALIBI_SLOPES[p_ // 2]
        krows = jnp.stack([sl * one, -sl * j_hi, -sl * j_lo], axis=0)
        kblocks += [jnp.zeros((A_DK, S), F32), krows, jnp.zeros((A_QKW - A_DK - 3, S), F32)]
    a_kpos = jnp.concatenate(kblocks, axis=0)
    v_ones = jnp.zeros((1, A_HEADS * V_AUG_W), F32).at[0, A_DV::V_AUG_W].set(1.0)
    return dict(cosq=cosq, sinq=sinq, cosk=cosk, sink=sink, eye=jnp.eye(C_ROPE, dtype=BF16),
                a_qpos=a_qpos, a_kpos=a_kpos, v_ones=v_ones)


def _layer(x2, B, S, l, w, tabs):
    T = B * S
    aq, akT, av, b_all, c_all, d_all = in_proj(x2, B, S, w["proj"], tabs)
    ya = diff_attention(aq, akT, av, w["lam"], w["subln_g"], l, B, S)
    of, ob = hgrn2(b_all, w["lb"], B, S)
    cq, ckT, cv = mla_prep(c_all, w["mla"], tabs, B, S)
    yc = mla_attention(cq, ckT, cv, B, S)
    hf, hb = rglru(d_all, w["rg"], B, S)
    x1, aff = merge(x2, ya, of, ob, b_all, yc, hf, hb, d_all, w["mrg"])

    cap = CAPACITY_FACTOR * T // N_EXPERTS
    affT = aff.T
    sel = select_tokens(affT.reshape(N_EXPERTS, T // 128, 128), cap, T).reshape(N_EXPERTS, T)
    order = jnp.argsort(1 - sel, axis=1, stable=True)[:, :cap].astype(jnp.int32)
    g = jnp.take_along_axis(affT, order, axis=1)
    ye = expert_ffn(order, x1, g, *w["ffn"])
    tt = _pick(T, CMB_TT)
    counts = jnp.sum(sel.reshape(N_EXPERTS, T // tt, tt), axis=2)
    tile_prefix = jnp.concatenate([jnp.zeros((N_EXPERTS, 1), jnp.int32), jnp.cumsum(counts, axis=1)], axis=1)
    return combine_norm(ye, sel.astype(F32), tile_prefix.T.reshape(-1).astype(jnp.int32), x1,
                        w["ln2_g"], w["ln2_b"], cap)


def _trunk(x, weights):
    B, S, _ = x.shape
    tabs = _rope_tables(S)
    x2 = x.reshape(B * S, D_MODEL)
    for l in range(DEPTH):
        x2 = _layer(x2, B, S, l, weights[l], tabs)
    return x2.reshape(B, S, D_MODEL)


def kernel(x_prompt, x_sample, w_in, diff_lambda, diff_subln_g, hgrn_lb_logits, hgrn_norm_g, mla_q_norm_g,
           mla_w_uq, mla_kv_norm_g, mla_w_ukv, rg_conv_w, rg_conv_b, rg_w_a, rg_b_a, rg_w_x, rg_b_x, rg_lambda,
           w_branch, w_out, ln_g, ln_b, w_router, w_e_gate, w_e_up, w_e_down):
    lb_all = jnp.cumsum(jax.nn.softmax(hgrn_lb_logits.astype(F32), axis=0), axis=0)
    lb_all = lb_all - lb_all[:1]
    p = dict(w_in=w_in, diff_lambda=diff_lambda, diff_subln_g=diff_subln_g, hgrn_norm_g=hgrn_norm_g,
             mla_q_norm_g=mla_q_norm_g, mla_w_uq=mla_w_uq, mla_kv_norm_g=mla_kv_norm_g, mla_w_ukv=mla_w_ukv,
             rg_conv_w=rg_conv_w, rg_conv_b=rg_conv_b, rg_w_a=rg_w_a, rg_b_a=rg_b_a, rg_w_x=rg_w_x,
             rg_b_x=rg_b_x, rg_lambda=rg_lambda, w_branch=w_branch, w_out=w_out, ln_g=ln_g, ln_b=ln_b,
             w_router=w_router, w_e_gate=w_e_gate, w_e_up=w_e_up, w_e_down=w_e_down)
    weights = [_prep_layer(l, p, lb_all) for l in range(DEPTH)]
    return (_trunk(x_prompt, weights), _trunk(x_sample, weights))
```

```python
import functools
import math

import numpy as np
import jax
import jax.numpy as jnp
from jax import lax
from jax.experimental import pallas as pl
from jax.experimental.pallas import tpu as pltpu

F32 = jnp.float32
BF16 = jnp.bfloat16

D_MODEL = 1024
DEPTH = 2
N_BRANCH = 4
BR_W = 256
A_HEADS = 4
A_DK = 32
A_DV = 64
B_HEADS = 4
B_DK = 64
C_HEADS = 4
C_NOPE = 32
C_ROPE = 16
C_QK = C_NOPE + C_ROPE
C_V = 64
C_Q_LORA = 192
C_KV_LORA = 128
C_PAD_W = 384
ROPE_BASE = 10000.0
D_BLOCKS = 4
CONV_W = 4
CONV_LEFT = 2
RG_C = 8.0
N_EXPERTS = 16
D_EXPERT = 2048
CAPACITY_FACTOR = 2
CHUNK = 128
LN_EPS = 1e-5
RMS_EPS = 1e-6
ALPHA = (2 * DEPTH) ** 0.25
ALIBI_SLOPES = tuple(2.0 ** (-8.0 * (h + 1) / A_HEADS) for h in range(A_HEADS))
A_NP = 2 * A_HEADS
A_QKW = 48
V_AUG_W = 128
ATT_TQ = 256
MLA_TQ = 512
ATT_TK = 2048

VMEM_LIMIT = 56 * 1024 * 1024

NT_DIMS = (((1,), (1,)), ((), ()))
TN_DIMS = (((0,), (0,)), ((), ()))


def _cparams(sem):
    return pltpu.CompilerParams(dimension_semantics=sem, vmem_limit_bytes=VMEM_LIMIT)


def _dot(a, b):
    return jnp.dot(a, b, preferred_element_type=F32)


def _dot_nt(a, b):
    return lax.dot_general(a, b, NT_DIMS, preferred_element_type=F32)


def _dot_tn(a, b):
    return lax.dot_general(a, b, TN_DIMS, preferred_element_type=F32)


def _dot_split(a, b_bf16):
    hi = a.astype(BF16)
    lo = (a - hi.astype(F32)).astype(BF16)
    return _dot(hi, b_bf16) + _dot(lo, b_bf16)


def _sigmoid(x):
    return 1.0 / (1.0 + jnp.exp(-x))


def _pick(n, pref):
    t = min(pref, n)
    while n % t:
        t //= 2
    return t


def _in_proj_kernel(x_ref, waq, wakT, wav, qpos, kpos, vones, wb, wc, wd, aq_o, akT_o, av_o, b_o, c_o, d_o):
    xb = x_ref[...].astype(BF16)
    aq_o[...] = (_dot(xb, waq[...]) * (A_DK ** -0.5) + qpos[...]).astype(BF16)
    akT_o[0] = (_dot_nt(wakT[...], xb) + kpos[...]).astype(BF16)
    av_o[0] = (_dot(xb, wav[...]) + vones[...]).astype(BF16)
    b_o[...] = _dot(xb, wb[...])
    c_o[...] = _dot(xb, wc[...])
    d_o[...] = _dot(xb, wd[...])


def in_proj(x2, B, S, w, tabs):
    T = B * S
    tm = _pick(S, 512)
    ns = S // tm
    aw = A_NP * A_QKW
    vw = A_HEADS * V_AUG_W
    full = lambda shape: pl.BlockSpec(shape, lambda i: (0,) * len(shape))
    rows = lambda wdt: pl.BlockSpec((tm, wdt), lambda i: (i, 0))
    return pl.pallas_call(
        _in_proj_kernel,
        grid=(T // tm,),
        in_specs=[rows(D_MODEL), full((D_MODEL, aw)), full((aw, D_MODEL)), full((D_MODEL, vw)),
                  pl.BlockSpec((tm, aw), lambda i: (i % ns, 0)),
                  pl.BlockSpec((aw, tm), lambda i: (0, i % ns)),
                  full((1, vw)),
                  full((D_MODEL, 5 * BR_W)), full((D_MODEL, C_PAD_W)), full((D_MODEL, 2 * BR_W))],
        out_specs=[rows(aw),
                   pl.BlockSpec((1, aw, tm), lambda i: (i // ns, 0, i % ns)),
                   pl.BlockSpec((1, tm, vw), lambda i: (i // ns, i % ns, 0)),
                   rows(5 * BR_W), rows(C_PAD_W), rows(2 * BR_W)],
        out_shape=[jax.ShapeDtypeStruct((T, aw), BF16),
                   jax.ShapeDtypeStruct((B, aw, S), BF16),
                   jax.ShapeDtypeStruct((B, S, vw), BF16),
                   jax.ShapeDtypeStruct((T, 5 * BR_W), F32),
                   jax.ShapeDtypeStruct((T, C_PAD_W), F32),
                   jax.ShapeDtypeStruct((T, 2 * BR_W), F32)],
        compiler_params=_cparams(("parallel",)),
        name="in_proj",
    )(x2, w["aq"], w["akT"], w["av"], tabs["a_qpos"], tabs["a_kpos"], tabs["v_ones"], w["b"], w["c"], w["d"])


def _softmax_tile(n_prob, qk, v_of, c_of, corr_of, m_s, acc_s):
    s_next = qk(0)
    for p in range(n_prob):
        s = s_next
        if p + 1 < n_prob:
            s_next = qk(p + 1)
        corr = corr_of(p)
        if corr is not None:
            s = s + corr
        c = c_of(p)
        m_old = m_s[p]
        m_new = jnp.maximum(m_old, jnp.max(s, axis=1, keepdims=True) + c)
        alpha = jnp.exp(m_old - m_new)
        pr = jnp.exp(s - (m_new - c))
        acc_s[p] = alpha * acc_s[p] + _dot(pr.astype(BF16), v_of(p))
        m_s[p] = m_new


def _diff_attn_kernel(lam_ref, q_ref, kT_ref, v_ref, g_ref, o_ref, qs, m_s, acc_s, *, tq, tk, nk, lambda_init):
    qi = pl.program_id(1)
    lane = lax.broadcasted_iota(jnp.int32, (1, A_NP * A_QKW), 1)
    is_pos = (lane % A_QKW) >= A_DK
    qa = q_ref[...].astype(F32)
    qneg = jnp.where(is_pos, -qa, qa)
    for p in range(A_NP):
        qs[p] = qa[:, A_QKW * p:A_QKW * (p + 1)].astype(BF16)
        qs[A_NP + p] = qneg[:, A_QKW * p:A_QKW * (p + 1)].astype(BF16)
    m_s[...] = jnp.full(m_s.shape, -jnp.inf, F32)
    acc_s[...] = jnp.zeros(acc_s.shape, F32)

    def tile(j, variant, straddle):
        j0 = pl.multiple_of(j * tk, tk)
        off = (qi * tq - j0).astype(F32)
        if straddle:
            d = (lax.broadcasted_iota(jnp.int32, (tq, tk), 0)
                 - lax.broadcasted_iota(jnp.int32, (tq, tk), 1)).astype(F32) + off
            dpos = jnp.maximum(d, 0.0)

        def qk(p):
            return _dot(qs[variant * A_NP + p], kT_ref[0, A_QKW * p:A_QKW * (p + 1), pl.ds(j0, tk)])

        _softmax_tile(
            A_NP, qk,
            v_of=lambda p: v_ref[0, pl.ds(j0, tk), V_AUG_W * (p // 2):V_AUG_W * (p // 2 + 1)],
            c_of=lambda p: (off if variant == 0 else -off) * ALIBI_SLOPES[p // 2],
            corr_of=lambda p: dpos * (-2.0 * ALIBI_SLOPES[p // 2]) if straddle else None,
            m_s=m_s, acc_s=acc_s)

    def left(j, carry):
        tile(j, 1, False)
        return carry

    def right(j, carry):
        tile(j, 0, False)
        return carry

    jd = qi // (tk // tq)
    lax.fori_loop(0, jd, left, 0)
    tile(jd, 0, True)
    lax.fori_loop(jd + 1, nk, right, 0)

    lam = lam_ref[0]
    for h in range(A_HEADS):
        a0 = acc_s[2 * h]
        a1 = acc_s[2 * h + 1]
        o = a0[:, :A_DV] / a0[:, A_DV:A_DV + 1] - lam * (a1[:, :A_DV] / a1[:, A_DV:A_DV + 1])
        ms = jnp.mean(o * o, axis=1, keepdims=True)
        o = o * lax.rsqrt(ms + RMS_EPS) * g_ref[...] * (1.0 - lambda_init)
        o_ref[:, A_DV * h:A_DV * (h + 1)] = o.astype(BF16)


def _att_tiles(S, tq_pref=ATT_TQ):
    tk = _pick(S, ATT_TK)
    tq = _pick(tk, tq_pref)
    return tq, tk


def diff_attention(aq, akT, av, lam, subln_g, layer_idx, B, S):
    tq, tk = _att_tiles(S)
    nq = S // tq
    aw = A_NP * A_QKW
    vw = A_HEADS * V_AUG_W
    lambda_init = 0.8 - 0.6 * math.exp(-0.3 * layer_idx)
    kern = functools.partial(_diff_attn_kernel, tq=tq, tk=tk, nk=S // tk, lambda_init=lambda_init)
    return pl.pallas_call(
        kern,
        grid=(B, nq),
        in_specs=[pl.BlockSpec(memory_space=pltpu.SMEM),
                  pl.BlockSpec((tq, aw), lambda b, i: (b * nq + i, 0)),
                  pl.BlockSpec((1, aw, S), lambda b, i: (b, 0, 0)),
                  pl.BlockSpec((1, S, vw), lambda b, i: (b, 0, 0)),
                  pl.BlockSpec((1, A_DV), lambda b, i: (0, 0))],
        out_specs=pl.BlockSpec((tq, BR_W), lambda b, i: (b * nq + i, 0)),
        out_shape=jax.ShapeDtypeStruct((B * S, BR_W), BF16),
        scratch_shapes=[pltpu.VMEM((2 * A_NP, tq, A_QKW), BF16),
                        pltpu.VMEM((A_NP, tq, 1), F32),
                        pltpu.VMEM((A_NP, tq, V_AUG_W), F32)],
        compiler_params=_cparams(("parallel", "arbitrary")),
        name="diff_attn",
    )(lam, aq, akT, av, subln_g)


def _mla_prep_kernel(c_ref, qg_ref, kvg_ref, wuq, wuq_rot, wukT, wuv, vones, cosq, sinq, cosk, sink, eye_ref,
                     q_o, kT_o, v_o):
    c = c_ref[...]
    cq = c[:, :C_Q_LORA]
    ckv = c[:, C_Q_LORA:C_Q_LORA + C_KV_LORA]
    ckr = c[:, C_Q_LORA + C_KV_LORA:C_Q_LORA + C_KV_LORA + C_ROPE]
    ckr_rot = c[:, C_Q_LORA + C_KV_LORA + C_ROPE:C_Q_LORA + C_KV_LORA + 2 * C_ROPE]
    nq = (cq * lax.rsqrt(jnp.mean(cq * cq, axis=1, keepdims=True) + RMS_EPS) * qg_ref[...]).astype(BF16)
    q = _dot(nq, wuq[...]) * cosq[...] + _dot(nq, wuq_rot[...]) * sinq[...]
    q_o[...] = (q * (C_QK ** -0.5)).astype(BF16)
    nkv = (ckv * lax.rsqrt(jnp.mean(ckv * ckv, axis=1, keepdims=True) + RMS_EPS) * kvg_ref[...]).astype(BF16)
    knT = _dot_nt(wukT[...], nkv).astype(BF16)
    v_o[0] = (_dot(nkv, wuv[...]) + vones[...]).astype(BF16)
    kr = (ckr * cosk[...] + ckr_rot * sink[...]).astype(BF16)
    krT = _dot_nt(eye_ref[...], kr).astype(BF16)
    for h in range(C_HEADS):
        kT_o[0, C_QK * h:C_QK * h + C_NOPE, :] = knT[C_NOPE * h:C_NOPE * (h + 1), :]
        kT_o[0, C_QK * h + C_NOPE:C_QK * (h + 1), :] = krT


def mla_prep(c_all, w, tabs, B, S):
    T = B * S
    tm = _pick(S, 512)
    ns = S // tm
    full = lambda shape: pl.BlockSpec(shape, lambda i: (0,) * len(shape))
    rows = lambda wdt: pl.BlockSpec((tm, wdt), lambda i: (i, 0))
    pos = lambda wdt: pl.BlockSpec((tm, wdt), lambda i: (i % ns, 0))
    qw = C_HEADS * C_QK
    vw = C_HEADS * V_AUG_W
    return pl.pallas_call(
        _mla_prep_kernel,
        grid=(T // tm,),
        in_specs=[rows(C_PAD_W), full((1, C_Q_LORA)), full((1, C_KV_LORA)), full((C_Q_LORA, qw)),
                  full((C_Q_LORA, qw)), full((C_HEADS * C_NOPE, C_KV_LORA)), full((C_KV_LORA, vw)),
                  full((1, vw)), pos(qw), pos(qw), pos(C_ROPE), pos(C_ROPE), full((C_ROPE, C_ROPE))],
        out_specs=[rows(qw),
                   pl.BlockSpec((1, qw, tm), lambda i: (i // ns, 0, i % ns)),
                   pl.BlockSpec((1, tm, vw), lambda i: (i // ns, i % ns, 0))],
        out_shape=[jax.ShapeDtypeStruct((T, qw), BF16),
                   jax.ShapeDtypeStruct((B, qw, S), BF16),
                   jax.ShapeDtypeStruct((B, S, vw), BF16)],
        compiler_params=_cparams(("parallel",)),
        name="mla_prep",
    )(c_all, w["q_norm_g"], w["kv_norm_g"], w["wuq"], w["wuq_rot"], w["wukT"], w["wuv"], tabs["v_ones"],
      tabs["cosq"], tabs["sinq"], tabs["cosk"], tabs["sink"], tabs["eye"])


def _mla_attn_kernel(q_ref, kT_ref, v_ref, o_ref, qs, m_s, acc_s, *, tk, nk):
    for h in range(C_HEADS):
        qs[h] = q_ref[:, C_QK * h:C_QK * (h + 1)]
    m_s[...] = jnp.full(m_s.shape, -jnp.inf, F32)
    acc_s[...] = jnp.zeros(acc_s.shape, F32)

    def tile(j, carry):
        j0 = pl.multiple_of(j * tk, tk)
        _softmax_tile(
            C_HEADS,
            qk=lambda h: _dot(qs[h], kT_ref[0, C_QK * h:C_QK * (h + 1), pl.ds(j0, tk)]),
            v_of=lambda h: v_ref[0, pl.ds(j0, tk), V_AUG_W * h:V_AUG_W * (h + 1)],
            c_of=lambda h: 0.0,
            corr_of=lambda h: None,
            m_s=m_s, acc_s=acc_s)
        return carry

    lax.fori_loop(0, nk, tile, 0)
    for h in range(C_HEADS):
        a = acc_s[h]
        o_ref[:, C_V * h:C_V * (h + 1)] = (a[:, :C_V] / a[:, C_V:C_V + 1]).astype(BF16)


def mla_attention(q, kT, v, B, S):
    tq, tk = _att_tiles(S, MLA_TQ)
    nq = S // tq
    qw = C_HEADS * C_QK
    vw = C_HEADS * V_AUG_W
    kern = functools.partial(_mla_attn_kernel, tk=tk, nk=S // tk)
    return pl.pallas_call(
        kern,
        grid=(B, nq),
        in_specs=[pl.BlockSpec((tq, qw), lambda b, i: (b * nq + i, 0)),
                  pl.BlockSpec((1, qw, S), lambda b, i: (b, 0, 0)),
                  pl.BlockSpec((1, S, vw), lambda b, i: (b, 0, 0))],
        out_specs=pl.BlockSpec((tq, BR_W), lambda b, i: (b * nq + i, 0)),
        out_shape=jax.ShapeDtypeStruct((B * S, BR_W), BF16),
        scratch_shapes=[pltpu.VMEM((C_HEADS, tq, C_QK), BF16),
                        pltpu.VMEM((C_HEADS, tq, 1), F32),
                        pltpu.VMEM((C_HEADS, tq, V_AUG_W), F32)],
        compiler_params=_cparams(("parallel", "arbitrary")),
        name="mla_attn",
    )(q, kT, v)


HGRN_LEVELS = (64, 32, 16, 8, 4, 2, 1)
N_HGRN_SUMS = len(HGRN_LEVELS) + 2


def _hgrn_constants():
    C = CHUNK
    t = np.arange(C)
    sums = np.zeros((2, N_HGRN_SUMS, C, C), np.float32)
    masks = np.zeros((2, len(HGRN_LEVELS) + 1, C, C), np.float32)
    for li, m in enumerate(HGRN_LEVELS):
        blk = t // m
        for tt in range(C):
            if blk[tt] % 2 == 1:
                sums[0, li, tt, blk[tt] * m:tt + 1] = 1.0
            else:
                sums[0, li, tt, tt + 1:(blk[tt] + 1) * m] = 1.0
        masks[0, li] = ((blk[:, None] % 2 == 1) & (blk[None, :] == blk[:, None] - 1)).astype(np.float32)
    masks[0, len(HGRN_LEVELS)] = np.eye(C, dtype=np.float32)
    sums[0, len(HGRN_LEVELS)] = (t[None, :] <= t[:, None]).astype(np.float32)
    sums[0, len(HGRN_LEVELS) + 1] = (t[None, :] > t[:, None]).astype(np.float32)
    sums[1] = sums[0][:, ::-1, ::-1]
    masks[1] = masks[0][:, ::-1, ::-1]
    return sums.reshape(2, N_HGRN_SUMS * C, C), np.concatenate([masks, masks], axis=3)


def _hgrn_kernel(qf_ref, ff_ref, vf_ref, qb_ref, fb_ref, vb_ref, lb_ref, sums_ref, mask_ref,
                 of_ref, ob_ref, st_ref):
    C = CHUNK
    nlev = len(HGRN_LEVELS)

    @pl.when(pl.program_id(1) == 0)
    def _():
        st_ref[...] = jnp.zeros(st_ref.shape, F32)

    lane = lax.broadcasted_iota(jnp.int32, (1, 2 * B_DK), 1)
    head_lanes = (lane < B_DK, lane >= B_DK)
    r_i = lax.broadcasted_iota(jnp.int32, (2 * B_DK, 2 * B_DK), 0)
    c_i = lax.broadcasted_iota(jnp.int32, (2 * B_DK, 2 * B_DK), 1)
    same_head = (r_i < B_DK) == (c_i < B_DK)

    for d, (q_ref, f_ref, v_ref, o_ref) in enumerate(((qf_ref, ff_ref, vf_ref, of_ref),
                                                      (qb_ref, fb_ref, vb_ref, ob_ref))):
        x = q_ref[...]
        q = x * _sigmoid(x)
        lb = lb_ref[d]
        f = lb + (1.0 - lb) * _sigmoid(f_ref[...])
        logf = jnp.log(f)
        k = 1.0 - f
        v = v_ref[...]
        hi = logf.astype(BF16)
        lo = (logf - hi.astype(F32)).astype(BF16)
        sums = sums_ref[d]
        E = _dot(sums, hi) + _dot(sums, lo)
        for p in range(B_HEADS // 2):
            sl = slice(2 * B_DK * p, 2 * B_DK * (p + 1))
            Qp, Kp, Vp = q[:, sl], k[:, sl], v[:, sl]

            def by_head(a):
                return jnp.concatenate([jnp.where(head_lanes[0], a, 0.0), jnp.where(head_lanes[1], a, 0.0)],
                                       axis=0).astype(BF16)

            A = jnp.zeros((C, 2 * C), F32)
            for l in range(nlev + 1):
                if l < nlev:
                    G = jnp.exp(E[C * l:C * (l + 1), sl])
                    QG = (Qp * G).astype(BF16)
                    KG = Kp * G
                else:
                    QG = Qp.astype(BF16)
                    KG = Kp
                A = A + mask_ref[d, l] * _dot_nt(QG, by_head(KG))
            o = _dot(A.astype(BF16), by_head(Vp))
            eb = jnp.exp(E[C * nlev:C * (nlev + 1), sl])
            st = st_ref[d, p]
            o = o + _dot_nt((Qp * eb).astype(BF16), st.astype(BF16))
            o_ref[:, sl] = o
            Kd = (Kp * jnp.exp(E[C * (nlev + 1):C * (nlev + 2), sl])).astype(BF16)
            row = C - 1 if d == 0 else 0
            g = eb[row:row + 1, :]
            upd = _dot_tn(Vp.astype(BF16), Kd)
            st_ref[d, p] = st * g + jnp.where(same_head, upd, 0.0)


def hgrn2(b_all, lb, B, S):
    T = B * S
    C = CHUNK
    nc = S // C
    sums, masks = _hgrn_constants()
    sums = jnp.asarray(sums, BF16)
    masks = jnp.asarray(masks, F32)
    fwd = lambda col: pl.BlockSpec((C, BR_W), lambda b, n: (b * nc + n, col))
    bwd = lambda col: pl.BlockSpec((C, BR_W), lambda b, n: (b * nc + nc - 1 - n, col))
    full = lambda shape: pl.BlockSpec(shape, lambda b, n: (0,) * len(shape))
    return pl.pallas_call(
        _hgrn_kernel,
        grid=(B, nc),
        in_specs=[fwd(0), fwd(1), fwd(3), bwd(0), bwd(2), bwd(3),
                  full((2, 1, BR_W)), full(sums.shape), full(masks.shape)],
        out_specs=[pl.BlockSpec((C, BR_W), lambda b, n: (b * nc + n, 0)),
                   pl.BlockSpec((C, BR_W), lambda b, n: (b * nc + nc - 1 - n, 0))],
        out_shape=[jax.ShapeDtypeStruct((T, BR_W), F32), jax.ShapeDtypeStruct((T, BR_W), F32)],
        scratch_shapes=[pltpu.VMEM((2, B_HEADS // 2, 2 * B_DK, 2 * B_DK), F32)],
        compiler_params=_cparams(("parallel", "arbitrary")),
        name="hgrn2",
    )(b_all, b_all, b_all, b_all, b_all, b_all, lb, sums, masks)


HALO = 8


def _rglru_kernel(xf_ref, xfp_ref, xfn_ref, xb_ref, xbp_ref, xbn_ref, cw_ref, cb_ref, wa_ref, ba_ref,
                  wx_ref, bx_ref, sp_ref, hf_ref, hb_ref, carry_ref, *, tc):
    n = pl.program_id(1)
    nc = pl.num_programs(1)

    @pl.when(n == 0)
    def _():
        carry_ref[...] = jnp.zeros(carry_ref.shape, F32)

    row = lax.broadcasted_iota(jnp.int32, (tc, 1), 0)
    for d, (x_ref, xp_ref, xn_ref, o_ref) in enumerate(((xf_ref, xfp_ref, xfn_ref, hf_ref),
                                                        (xb_ref, xbp_ref, xbn_ref, hb_ref))):
        cidx = n if d == 0 else nc - 1 - n
        prev = jnp.where(cidx > 0, xp_ref[...], 0.0)
        nxt = jnp.where(cidx < nc - 1, xn_ref[...], 0.0)
        ext = jnp.concatenate([prev, x_ref[...], nxt], axis=0)
        xc = cb_ref[...]
        for j in range(CONV_W):
            off = HALO - CONV_LEFT + j
            xc = xc + ext[off:off + tc, :] * cw_ref[j:j + 1, :]
        xcb = xc.astype(BF16)
        r = _sigmoid(_dot(xcb, wa_ref[d]) + ba_ref[d])
        i = _sigmoid(_dot(xcb, wx_ref[d]) + bx_ref[d])
        log_a = (-RG_C) * r * sp_ref[d]
        a = jnp.exp(log_a)
        u = jnp.sqrt(1.0 - jnp.exp(2.0 * log_a)) * (i * xc)
        sh = 1
        while sh < tc:
            if d == 0:
                valid = row >= sh
                amt = sh
            else:
                valid = row < tc - sh
                amt = tc - sh
            a_sh = jnp.where(valid, pltpu.roll(a, amt, 0), 1.0)
            u_sh = jnp.where(valid, pltpu.roll(u, amt, 0), 0.0)
            u = a * u_sh + u
            a = a * a_sh
            sh *= 2
        h = u + a * carry_ref[d]
        o_ref[...] = h
        last = tc - 1 if d == 0 else 0
        carry_ref[d] = h[last:last + 1, :]


def rglru(d_all, w, B, S):
    T = B * S
    tc = _pick(S, 512)
    nc = S // tc
    hb = tc // HALO
    nrow8 = T // HALO
    fc = lambda b, n: b * nc + n
    bc = lambda b, n: b * nc + nc - 1 - n
    cur = lambda f: pl.BlockSpec((tc, BR_W), lambda b, n: (f(b, n), 0))
    prv = lambda f: pl.BlockSpec((HALO, BR_W), lambda b, n: (jnp.maximum(f(b, n) * hb - 1, 0), 0))
    nxt = lambda f: pl.BlockSpec((HALO, BR_W), lambda b, n: (jnp.minimum((f(b, n) + 1) * hb, nrow8 - 1), 0))
    full = lambda shape: pl.BlockSpec(shape, lambda b, n: (0,) * len(shape))
    kern = functools.partial(_rglru_kernel, tc=tc)
    return pl.pallas_call(
        kern,
        grid=(B, nc),
        in_specs=[cur(fc), prv(fc), nxt(fc), cur(bc), prv(bc), nxt(bc),
                  full((CONV_W, BR_W)), full((1, BR_W)), full((2, BR_W, BR_W)), full((2, 1, BR_W)),
                  full((2, BR_W, BR_W)), full((2, 1, BR_W)), full((2, 1, BR_W))],
        out_specs=[cur(fc), cur(bc)],
        out_shape=[jax.ShapeDtypeStruct((T, BR_W), F32), jax.ShapeDtypeStruct((T, BR_W), F32)],
        scratch_shapes=[pltpu.VMEM((2, 1, BR_W), F32)],
        compiler_params=_cparams(("parallel", "arbitrary")),
        name="rglru",
    )(d_all, d_all, d_all, d_all, d_all, d_all, w["conv_w"], w["conv_b"], w["wa"], w["ba"], w["wx"],
      w["bx"], w["sp"])


def _layer_norm(z, g, b):
    mu = jnp.mean(z, axis=1, keepdims=True)
    zc = z - mu
    var = jnp.mean(zc * zc, axis=1, keepdims=True)
    return zc * lax.rsqrt(var + LN_EPS) * g + b


def _merge_kernel(x_ref, ya_ref, of_ref, ob_ref, bg_ref, yc_ref, hf_ref, hb_ref, dg_ref, wg_ref, wbr_ref,
                  wo_ref, lng_ref, lnb_ref, hg_ref, havg_ref, wr_ref, x1_ref, aff_ref):
    x = x_ref[...]
    xb = x.astype(BF16)
    o = of_ref[...] + ob_ref[...]
    ms = _dot_split(o * o, havg_ref[...])
    bg = bg_ref[...]
    yb = o * lax.rsqrt(ms + RMS_EPS) * hg_ref[...] * (bg * _sigmoid(bg))
    dg = dg_ref[...]
    gelu = 0.5 * dg * (1.0 + jnp.tanh(0.7978845608028654 * (dg + 0.044715 * dg * dg * dg)))
    yd = (hf_ref[...] + hb_ref[...]) * gelu
    ys = (ya_ref[...], yb.astype(BF16), yc_ref[...], yd.astype(BF16))
    mix = None
    for i in range(N_BRANCH):
        gate = _sigmoid(_dot(xb, wg_ref[:, D_MODEL * i:D_MODEL * (i + 1)]))
        term = gate * _dot(ys[i], wbr_ref[i])
        mix = term if mix is None else mix + term
    z = ALPHA * x + _dot(mix.astype(BF16), wo_ref[...])
    x1 = _layer_norm(z, lng_ref[...], lnb_ref[...])
    x1_ref[...] = x1
    x1h = x1.astype(BF16)
    x1l = (x1 - x1h.astype(F32)).astype(BF16)
    logits = _dot_nt(x1h, wr_ref[0]) + (_dot_nt(x1l, wr_ref[0]) + _dot_nt(x1h, wr_ref[1]))
    e = jnp.exp(logits - jnp.max(logits, axis=1, keepdims=True))
    aff_ref[...] = e / jnp.sum(e, axis=1, keepdims=True)


def merge(x2, ya, of, ob, b_all, yc, hf, hb, d_all, w):
    T = x2.shape[0]
    tm = _pick(T, 512)
    full = lambda shape: pl.BlockSpec(shape, lambda i: (0,) * len(shape))
    rows = lambda wdt: pl.BlockSpec((tm, wdt), lambda i: (i, 0))
    colblk = lambda c: pl.BlockSpec((tm, BR_W), lambda i: (i, c))
    return pl.pallas_call(
        _merge_kernel,
        grid=(T // tm,),
        in_specs=[rows(D_MODEL), rows(BR_W), rows(BR_W), rows(BR_W), colblk(4), rows(BR_W), rows(BR_W),
                  rows(BR_W), colblk(1), full((D_MODEL, N_BRANCH * D_MODEL)),
                  full((N_BRANCH, BR_W, D_MODEL)), full((D_MODEL, D_MODEL)), full((1, D_MODEL)),
                  full((1, D_MODEL)), full((1, BR_W)), full((BR_W, BR_W)), full((2, N_EXPERTS, D_MODEL))],
        out_specs=[rows(D_MODEL), rows(N_EXPERTS)],
        out_shape=[jax.ShapeDtypeStruct((T, D_MODEL), F32), jax.ShapeDtypeStruct((T, N_EXPERTS), F32)],
        compiler_params=_cparams(("parallel",)),
        name="merge",
    )(x2, ya, of, ob, b_all, yc, hf, hb, d_all, w["gate"], w["branch"], w["out"], w["ln_g"], w["ln_b"],
      w["hgrn_norm_g"], w["head_avg"], w["router"])


def _select_kernel(aff_ref, sel_ref, *, cap, n_tok):
    aff = aff_ref[...]
    v = lax.bitcast_convert_type(aff, jnp.int32)
    E, R, L = aff.shape

    def count(pred):
        c = jnp.sum(pred.astype(F32), axis=1)
        return jnp.sum(c, axis=1, keepdims=True)

    def value_step(i, thr):
        cand = thr | jnp.left_shift(jnp.int32(1), 30 - i)
        ok = count(v >= cand[:, :, None]) >= cap
        return jnp.where(ok, cand, thr)

    thr = lax.fori_loop(0, 31, value_step, jnp.zeros((E, 1), jnp.int32))
    thr3 = thr[:, :, None]
    gt = v > thr3
    eq = v == thr3
    need = cap - count(gt)
    tok = (lax.broadcasted_iota(jnp.int32, (E, R, L), 1) * L
           + lax.broadcasted_iota(jnp.int32, (E, R, L), 2))
    nbits = max(1, int(n_tok).bit_length())

    def index_step(i, bound):
        cand = bound | jnp.left_shift(jnp.int32(1), nbits - 1 - i)
        ok = count(eq & (tok < cand[:, :, None])) <= need
        return jnp.where(ok, cand, bound)

    bound = lax.fori_loop(0, nbits, index_step, jnp.zeros((E, 1), jnp.int32))
    sel = gt | (eq & (tok < bound[:, :, None]))
    sel_ref[...] = sel.astype(jnp.int32)


def select_tokens(affT3, cap, n_tok):
    E, R, L = affT3.shape
    kern = functools.partial(_select_kernel, cap=cap, n_tok=n_tok)
    return pl.pallas_call(
        kern,
        grid=(1,),
        in_specs=[pl.BlockSpec((E, R, L), lambda i: (0, 0, 0))],
        out_specs=pl.BlockSpec((E, R, L), lambda i: (0, 0, 0)),
        out_shape=jax.ShapeDtypeStruct((E, R, L), jnp.int32),
        compiler_params=_cparams(("arbitrary",)),
        name="select_tokens",
    )(affT3)


FFN_FCHUNK = 512


def _ffn_kernel(idx_hbm, x_hbm, g_ref, wg_ref, wu_ref, wd_ref, ye_ref, idx_smem, xbuf, idx_sem, row_sem,
                *, tm, n_blocks):
    pair = pl.program_id(0) * pl.num_programs(1) + pl.program_id(1)
    last = n_blocks - 1

    def idx_copy(b, s):
        return pltpu.make_async_copy(idx_hbm.at[b], idx_smem.at[s], idx_sem.at[s])

    def row_copy(src_row, s, j):
        return pltpu.make_async_copy(x_hbm.at[pl.ds(src_row, 1), :], xbuf.at[s, pl.ds(j, 1), :],
                                     row_sem.at[s])

    def wait_rows(s):
        pltpu.make_async_copy(x_hbm.at[pl.ds(0, tm), :], xbuf.at[s], row_sem.at[s]).wait()

    @pl.when(pair == 0)
    def _():
        first = idx_copy(0, 0)
        first.start()
        first.wait()

        def issue(j, carry):
            row_copy(idx_smem[0, j], 0, j).start()
            return carry

        lax.fori_loop(0, tm, issue, 0)
        idx_copy(jnp.minimum(1, last), 1).start()

    n_chunks = D_EXPERT // FFN_FCHUNK
    per_chunk = tm // n_chunks
    for slot in range(2):
        nslot = 1 - slot
        blk = 2 * pair + slot
        idx_copy(0, nslot).wait()
        wait_rows(slot)
        xb = xbuf[slot].astype(BF16)
        acc = jnp.zeros((tm, D_MODEL), F32)
        for c in range(n_chunks):
            for j in range(per_chunk * c, per_chunk * (c + 1)):
                row_copy(idx_smem[nslot, j], nslot, j).start()
            fs = slice(FFN_FCHUNK * c, FFN_FCHUNK * (c + 1))
            hg = _dot(xb, wg_ref[0, :, fs])
            hu = _dot(xb, wu_ref[0, :, fs])
            h = (hg * _sigmoid(hg) * hu).astype(BF16)
            acc = acc + _dot(h, wd_ref[0, fs, :])
        ye_ref[tm * slot:tm * (slot + 1), :] = acc * g_ref[tm * slot:tm * (slot + 1), :]
        idx_copy(jnp.minimum(blk + 2, last), slot).start()

    @pl.when(2 * pair + 1 == last)
    def _():
        wait_rows(0)
        idx_copy(0, 1).wait()


def expert_ffn(idx, x1, g, wg, wu, wd):
    E, cap = idx.shape
    tm = _pick(cap // 2, 512)
    R = cap // tm
    RP = R // 2
    kern = functools.partial(_ffn_kernel, tm=tm, n_blocks=E * R)
    return pl.pallas_call(
        kern,
        grid=(E, RP),
        in_specs=[pl.BlockSpec(memory_space=pl.ANY),
                  pl.BlockSpec(memory_space=pl.ANY),
                  pl.BlockSpec((2 * tm, 1), lambda e, r: (e * RP + r, 0)),
                  pl.BlockSpec((1, D_MODEL, D_EXPERT), lambda e, r: (e, 0, 0)),
                  pl.BlockSpec((1, D_MODEL, D_EXPERT), lambda e, r: (e, 0, 0)),
                  pl.BlockSpec((1, D_EXPERT, D_MODEL), lambda e, r: (e, 0, 0))],
        out_specs=pl.BlockSpec((2 * tm, D_MODEL), lambda e, r: (e * RP + r, 0)),
        out_shape=jax.ShapeDtypeStruct((E * cap, D_MODEL), F32),
        scratch_shapes=[pltpu.SMEM((2, tm), jnp.int32),
                        pltpu.VMEM((2, tm, D_MODEL), F32),
                        pltpu.SemaphoreType.DMA((2,)),
                        pltpu.SemaphoreType.DMA((2,))],
        compiler_params=pltpu.CompilerParams(dimension_semantics=("arbitrary", "arbitrary"),
                                             vmem_limit_bytes=VMEM_LIMIT, disable_bounds_checks=True),
        name="expert_ffn",
    )(idx.reshape(E * R, tm), x1, g.reshape(E * cap, 1), wg, wu, wd)


CMB_TT = 128
CMB_PIECE = 32
CMB_CHUNK = 8
SUBLANES = 8


def _combine_kernel(p_ref, ye_hbm, sel_ref, x_ref, tri_ref, g_ref, b_ref, o_ref,
                    buf, meta, nfill, pre_ref, oh_ref, acc_ref, sem, *, cap, n_rows, n_tiles, max_pieces):
    i = pl.program_id(0)
    slot = i % 2
    E = N_EXPERTS
    PC = CMB_PIECE

    def piece_copy(src_row, s, k):
        return pltpu.make_async_copy(ye_hbm.at[pl.ds(src_row, PC), :], buf.at[s, pl.ds(k * PC, PC), :],
                                     sem.at[s])

    def issue_tile(tile, s):
        def expert_body(e, fill):
            lo = p_ref[tile * E + e]
            s0 = e * cap + lo
            s1 = e * cap + p_ref[(tile + 1) * E + e]
            a0 = (s0 // SUBLANES) * SUBLANES

            def piece(q, fill):
                start = a0 + PC * q
                start_c = pl.multiple_of(jnp.minimum(start, n_rows - PC), SUBLANES)
                piece_copy(start_c, s, fill).start()
                meta[s, 0, fill] = e
                meta[s, 1, fill] = start_c - s0
                meta[s, 2, fill] = start - start_c
                return fill + 1

            n_pieces = jnp.where(s1 > s0, (s1 - a0 + PC - 1) // PC, 0)
            return lax.fori_loop(0, n_pieces, piece, fill)

        nfill[s] = lax.fori_loop(0, E, expert_body, 0)

    @pl.when(i == 0)
    def _():
        buf[...] = jnp.zeros(buf.shape, F32)

        def clear(k, carry):
            for s in range(2):
                for f in range(3):
                    meta[s, f, k] = 0
            return carry

        lax.fori_loop(0, max_pieces, clear, 0)
        issue_tile(0, 0)

    @pl.when(i + 1 < n_tiles)
    def _():
        issue_tile(i + 1, 1 - slot)

    fill = nfill[slot]

    def wait_piece(k, carry):
        piece_copy(0, slot, k).wait()
        return carry

    lax.fori_loop(0, fill, wait_piece, 0)

    selv = sel_ref[...]
    pre_ref[...] = _dot(selv.astype(BF16), tri_ref[...])
    acc_ref[...] = jnp.zeros(acc_ref.shape, F32)
    row = lax.broadcasted_iota(jnp.int32, (PC, 1), 0)

    def chunk(c, carry):
        for k in range(CMB_CHUNK):
            pidx = c * CMB_CHUNK + k
            live = pidx < fill
            e = jnp.where(live, meta[slot, 0, pidx], 0)
            rank = jnp.where((row >= meta[slot, 2, pidx]) & live, row + meta[slot, 1, pidx], -1).astype(F32)
            hit = (pre_ref[pl.ds(e, 1), :] == rank) & (sel_ref[pl.ds(e, 1), :] > 0.5)
            oh_ref[PC * k:PC * (k + 1), :] = jnp.where(hit, 1.0, 0.0).astype(BF16)
        rows = buf[slot, pl.ds(pl.multiple_of(c * (CMB_CHUNK * PC), CMB_CHUNK * PC), CMB_CHUNK * PC), :]
        hi = rows.astype(BF16)
        lo = (rows - hi.astype(F32)).astype(BF16)
        oh = oh_ref[...]
        acc_ref[...] += _dot_tn(oh, hi) + _dot_tn(oh, lo)
        return carry

    lax.fori_loop(0, (fill + CMB_CHUNK - 1) // CMB_CHUNK, chunk, 0)
    o_ref[...] = _layer_norm(ALPHA * x_ref[...] + acc_ref[...], g_ref[...], b_ref[...])


def combine_norm(ye, sel_f, tile_prefix, x1, g, b, cap):
    T = x1.shape[0]
    tt = _pick(T, CMB_TT)
    n_tiles = T // tt
    max_pieces = N_EXPERTS * (tt // CMB_PIECE + 1)
    max_pieces = ((max_pieces + CMB_CHUNK - 1) // CMB_CHUNK) * CMB_CHUNK
    tri = jnp.asarray(np.triu(np.ones((tt, tt), np.float32), 1), BF16)
    kern = functools.partial(_combine_kernel, cap=cap, n_rows=ye.shape[0], n_tiles=n_tiles,
                             max_pieces=max_pieces)
    grid_spec = pltpu.PrefetchScalarGridSpec(
        num_scalar_prefetch=1,
        grid=(n_tiles,),
        in_specs=[pl.BlockSpec(memory_space=pl.ANY),
                  pl.BlockSpec((N_EXPERTS, tt), lambda i, p: (0, i)),
                  pl.BlockSpec((tt, D_MODEL), lambda i, p: (i, 0)),
                  pl.BlockSpec((tt, tt), lambda i, p: (0, 0)),
                  pl.BlockSpec((1, D_MODEL), lambda i, p: (0, 0)),
                  pl.BlockSpec((1, D_MODEL), lambda i, p: (0, 0))],
        out_specs=pl.BlockSpec((tt, D_MODEL), lambda i, p: (i, 0)),
        scratch_shapes=[pltpu.VMEM((2, max_pieces * CMB_PIECE, D_MODEL), F32),
                        pltpu.SMEM((2, 3, max_pieces), jnp.int32),
                        pltpu.SMEM((2,), jnp.int32),
                        pltpu.VMEM((N_EXPERTS, tt), F32),
                        pltpu.VMEM((CMB_CHUNK * CMB_PIECE, tt), BF16),
                        pltpu.VMEM((tt, D_MODEL), F32),
                        pltpu.SemaphoreType.DMA((2,))])
    return pl.pallas_call(
        kern,
        grid_spec=grid_spec,
        out_shape=jax.ShapeDtypeStruct((T, D_MODEL), F32),
        compiler_params=pltpu.CompilerParams(dimension_semantics=("arbitrary",), vmem_limit_bytes=VMEM_LIMIT,
                                             disable_bounds_checks=True),
        name="combine_norm",
    )(tile_prefix, ye, sel_f, x1, tri, g, b)


def _rot_cols(w, start):
    half = C_ROPE // 2
    return jnp.concatenate([-w[:, start + half:start + C_ROPE], w[:, start:start + half]], axis=1)


def _block_diag(w):
    n, c, _ = w.shape
    out = jnp.zeros((n * c, n * c), w.dtype)
    for i in range(n):
        out = out.at[i * c:(i + 1) * c, i * c:(i + 1) * c].set(w[i])
    return out


def _prep_layer(l, p, lb_all):
    w_in = p["w_in"][l]
    o = 0
    aq = w_in[:, o:o + BR_W]; o += BR_W
    ak = w_in[:, o:o + BR_W]; o += BR_W
    av = w_in[:, o:o + BR_W]; o += BR_W
    wb = w_in[:, o:o + 5 * BR_W]; o += 5 * BR_W
    c0 = o
    wcq = w_in[:, o:o + C_Q_LORA]; o += C_Q_LORA
    wckv = w_in[:, o:o + C_KV_LORA]; o += C_KV_LORA
    wckr = w_in[:, o:o + C_ROPE]; o += C_ROPE
    wd = w_in[:, o:o + 2 * BR_W]; o += 2 * BR_W
    wgate = w_in[:, o:o + N_BRANCH * D_MODEL]
    used = C_Q_LORA + C_KV_LORA + 2 * C_ROPE
    wc = jnp.concatenate([wcq, wckv, wckr, _rot_cols(w_in, c0 + C_Q_LORA + C_KV_LORA),
                          jnp.zeros((D_MODEL, C_PAD_W - used), F32)], axis=1)
    padq = jnp.zeros((D_MODEL, A_QKW - A_DK), F32)
    padv = jnp.zeros((D_MODEL, V_AUG_W - A_DV), F32)
    aq_w = jnp.concatenate([t for p_ in range(A_NP) for t in (aq[:, A_DK * p_:A_DK * (p_ + 1)], padq)], axis=1)
    ak_w = jnp.concatenate([t for p_ in range(A_NP) for t in (ak[:, A_DK * p_:A_DK * (p_ + 1)], padq)], axis=1)
    av_w = jnp.concatenate([t for h in range(A_HEADS) for t in (av[:, A_DV * h:A_DV * (h + 1)], padv)], axis=1)
    proj = dict(aq=aq_w.astype(BF16), akT=ak_w.T.astype(BF16), av=av_w.astype(BF16), b=wb.astype(BF16),
                c=wc.astype(BF16), d=wd.astype(BF16))

    wuq = p["mla_w_uq"][l]
    wuq_rot = jnp.zeros_like(wuq)
    for h in range(C_HEADS):
        r0 = C_QK * h + C_NOPE
        wuq_rot = wuq_rot.at[:, r0:r0 + C_ROPE].set(_rot_cols(wuq, r0))
    wukv = p["mla_w_ukv"][l].reshape(C_KV_LORA, C_HEADS, C_NOPE + C_V)
    wuv = jnp.concatenate([wukv[:, :, C_NOPE:], jnp.zeros((C_KV_LORA, C_HEADS, V_AUG_W - C_V), F32)], axis=2)
    mla = dict(q_norm_g=p["mla_q_norm_g"][l][None, :], kv_norm_g=p["mla_kv_norm_g"][l][None, :],
               wuq=wuq.astype(BF16), wuq_rot=wuq_rot.astype(BF16),
               wukT=wukv[:, :, :C_NOPE].reshape(C_KV_LORA, C_HEADS * C_NOPE).T.astype(BF16),
               wuv=wuv.reshape(C_KV_LORA, C_HEADS * V_AUG_W).astype(BF16))

    rg = dict(conv_w=p["rg_conv_w"][l], conv_b=p["rg_conv_b"][l][None, :],
              wa=jnp.stack([_block_diag(p["rg_w_a"][l, d]) for d in range(2)]).astype(BF16),
              ba=p["rg_b_a"][l][:, None, :],
              wx=jnp.stack([_block_diag(p["rg_w_x"][l, d]) for d in range(2)]).astype(BF16),
              bx=p["rg_b_x"][l][:, None, :],
              sp=jax.nn.softplus(-p["rg_lambda"][l].astype(F32))[:, None, :])

    head = np.arange(BR_W) // B_DK
    wr_hi = p["w_router"][l].astype(BF16)
    mrg = dict(gate=wgate.astype(BF16), branch=p["w_branch"][l].astype(BF16), out=p["w_out"][l].astype(BF16),
               ln_g=p["ln_g"][l, 0][None, :], ln_b=p["ln_b"][l, 0][None, :],
               hgrn_norm_g=p["hgrn_norm_g"][l][None, :],
               head_avg=jnp.asarray((head[:, None] == head[None, :]).astype(np.float32) / B_DK, BF16),
               router=jnp.stack([wr_hi.T, (p["w_router"][l] - wr_hi.astype(F32)).astype(BF16).T]))

    lp = p["diff_lambda"][l].astype(F32)
    lambda_init = 0.8 - 0.6 * math.exp(-0.3 * l)
    lam = (jnp.exp(jnp.sum(lp[0] * lp[1])) - jnp.exp(jnp.sum(lp[2] * lp[3])) + lambda_init).reshape(1)
    return dict(proj=proj, mla=mla, rg=rg, mrg=mrg, lam=lam, subln_g=p["diff_subln_g"][l][None, :],
                lb=lb_all[l][:, None, :],
                ffn=(p["w_e_gate"][l].astype(BF16), p["w_e_up"][l].astype(BF16), p["w_e_down"][l].astype(BF16)),
                ln2_g=p["ln_g"][l, 1][None, :], ln2_b=p["ln_b"][l, 1][None, :])


def _rope_tables(S):
    half = C_ROPE // 2
    inv = ROPE_BASE ** (-jnp.arange(0, C_ROPE, 2, dtype=F32) / C_ROPE)
    ang = jnp.arange(S, dtype=F32)[:, None] * inv[None, :]
    cos, sin = jnp.cos(ang), jnp.sin(ang)
    cosk = jnp.concatenate([cos, cos], axis=1)
    sink = jnp.concatenate([sin, sin], axis=1)
    ones = jnp.ones((S, C_NOPE), F32)
    zeros = jnp.zeros((S, C_NOPE), F32)
    cosq = jnp.concatenate([jnp.concatenate([ones, cosk], axis=1)] * C_HEADS, axis=1)
    sinq = jnp.concatenate([jnp.concatenate([zeros, sink], axis=1)] * C_HEADS, axis=1)
    tq, tk = _att_tiles(S)
    pos = jnp.arange(S, dtype=jnp.int32)
    i_rel = (pos % tq).astype(F32)
    j_rel = pos % tk
    j_hi = (16 * (j_rel // 16)).astype(F32)
    j_lo = (j_rel % 16).astype(F32)
    one = jnp.ones((S,), F32)
    qcols = jnp.stack([i_rel, one, one], axis=1)
    qpad = jnp.zeros((S, A_QKW - A_DK - 3), F32)
    qblock = jnp.concatenate([jnp.zeros((S, A_DK), F32), qcols, qpad], axis=1)
    a_qpos = jnp.concatenate([qblock] * A_NP, axis=1)
    kblocks = []
    for p_ in range(A_NP):
        sl = ALIBI_SLOPES[p_ // 2]
        krows = jnp.stack([sl * one, -sl * j_hi, -sl * j_lo], axis=0)
        kblocks += [jnp.zeros((A_DK, S), F32), krows, jnp.zeros((A_QKW - A_DK - 3, S), F32)]
    a_kpos = jnp.concatenate(kblocks, axis=0)
    v_ones = jnp.zeros((1, A_HEADS * V_AUG_W), F32).at[0, A_DV::V_AUG_W].set(1.0)
    return dict(cosq=cosq, sinq=sinq, cosk=cosk, sink=sink, eye=jnp.eye(C_ROPE, dtype=BF16),
                a_qpos=a_qpos, a_kpos=a_kpos, v_ones=v_ones)


def _layer(x2, B, S, l, w, tabs):
    T = B * S
    aq, akT, av, b_all, c_all, d_all = in_proj(x2, B, S, w["proj"], tabs)
    ya = diff_attention(aq, akT, av, w["lam"], w["subln_g"], l, B, S)
    of, ob = hgrn2(b_all, w["lb"], B, S)
    cq, ckT, cv = mla_prep(c_all, w["mla"], tabs, B, S)
    yc = mla_attention(cq, ckT, cv, B, S)
    hf, hb = rglru(d_all, w["rg"], B, S)
    x1, aff = merge(x2, ya, of, ob, b_all, yc, hf, hb, d_all, w["mrg"])

    cap = CAPACITY_FACTOR * T // N_EXPERTS
    affT = aff.T
    sel = select_tokens(affT.reshape(N_EXPERTS, T // 128, 128), cap, T).reshape(N_EXPERTS, T)
    tok = lax.broadcasted_iota(jnp.int32, sel.shape, 1)
    order = jnp.sort(jnp.where(sel > 0, tok, tok + T), axis=1)[:, :cap]
    g = jnp.take_along_axis(affT, order, axis=1)
    ye = expert_ffn(order, x1, g, *w["ffn"])
    tt = _pick(T, CMB_TT)
    counts = jnp.sum(sel.reshape(N_EXPERTS, T // tt, tt), axis=2)
    tile_prefix = jnp.concatenate([jnp.zeros((N_EXPERTS, 1), jnp.int32), jnp.cumsum(counts, axis=1)], axis=1)
    return combine_norm(ye, sel.astype(F32), tile_prefix.T.reshape(-1).astype(jnp.int32), x1,
                        w["ln2_g"], w["ln2_b"], cap)


def _trunk(x, weights):
    B, S, _ = x.shape
    tabs = _rope_tables(S)
    x2 = x.reshape(B * S, D_MODEL)
    for l in range(DEPTH):
        x2 = _layer(x2, B, S, l, weights[l], tabs)
    return x2.reshape(B, S, D_MODEL)


def kernel(x_prompt, x_sample, w_in, diff_lambda, diff_subln_g, hgrn_lb_logits, hgrn_norm_g, mla_q_norm_g,
           mla_w_uq, mla_kv_norm_g, mla_w_ukv, rg_conv_w, rg_conv_b, rg_w_a, rg_b_a, rg_w_x, rg_b_x, rg_lambda,
           w_branch, w_out, ln_g, ln_b, w_router, w_e_gate, w_e_up, w_e_down):
    lb_all = jnp.cumsum(jax.nn.softmax(hgrn_lb_logits.astype(F32), axis=0), axis=0)
    lb_all = lb_all - lb_all[:1]
    p = dict(w_in=w_in, diff_lambda=diff_lambda, diff_subln_g=diff_subln_g, hgrn_norm_g=hgrn_norm_g,
             mla_q_norm_g=mla_q_norm_g, mla_w_uq=mla_w_uq, mla_kv_norm_g=mla_kv_norm_g, mla_w_ukv=mla_w_ukv,
             rg_conv_w=rg_conv_w, rg_conv_b=rg_conv_b, rg_w_a=rg_w_a, rg_b_a=rg_b_a, rg_w_x=rg_w_x,
             rg_b_x=rg_b_x, rg_lambda=rg_lambda, w_branch=w_branch, w_out=w_out, ln_g=ln_g, ln_b=ln_b,
             w_router=w_router, w_e_gate=w_e_gate, w_e_up=w_e_up, w_e_down=w_e_down)
    weights = [_prep_layer(l, p, lb_all) for l in range(DEPTH)]
    return (_trunk(x_prompt, weights), _trunk(x_sample, weights))
```

```python
import functools
import math

import numpy as np
import jax
import jax.numpy as jnp
from jax import lax
from jax.experimental import pallas as pl
from jax.experimental.pallas import tpu as pltpu

F32 = jnp.float32
BF16 = jnp.bfloat16

D_MODEL = 1024
DEPTH = 2
N_BRANCH = 4
BR_W = 256
A_HEADS = 4
A_DK = 32
A_DV = 64
B_HEADS = 4
B_DK = 64
C_HEADS = 4
C_NOPE = 32
C_ROPE = 16
C_QK = C_NOPE + C_ROPE
C_V = 64
C_Q_LORA = 192
C_KV_LORA = 128
C_PAD_W = 384
ROPE_BASE = 10000.0
D_BLOCKS = 4
CONV_W = 4
CONV_LEFT = 2
RG_C = 8.0
N_EXPERTS = 16
D_EXPERT = 2048
CAPACITY_FACTOR = 2
CHUNK = 128
LN_EPS = 1e-5
RMS_EPS = 1e-6
ALPHA = (2 * DEPTH) ** 0.25
ALIBI_SLOPES = tuple(2.0 ** (-8.0 * (h + 1) / A_HEADS) for h in range(A_HEADS))
A_NP = 2 * A_HEADS
A_QKW = 48
V_AUG_W = 128
ATT_TQ = 256
MLA_TQ = 512
ATT_TK = 2048

VMEM_LIMIT = 56 * 1024 * 1024

NT_DIMS = (((1,), (1,)), ((), ()))
TN_DIMS = (((0,), (0,)), ((), ()))


def _cparams(sem):
    return pltpu.CompilerParams(dimension_semantics=sem, vmem_limit_bytes=VMEM_LIMIT)


def _dot(a, b):
    return jnp.dot(a, b, preferred_element_type=F32)


def _dot_nt(a, b):
    return lax.dot_general(a, b, NT_DIMS, preferred_element_type=F32)


def _dot_tn(a, b):
    return lax.dot_general(a, b, TN_DIMS, preferred_element_type=F32)


def _dot_split(a, b_bf16):
    hi = a.astype(BF16)
    lo = (a - hi.astype(F32)).astype(BF16)
    return _dot(hi, b_bf16) + _dot(lo, b_bf16)


def _sigmoid(x):
    return 1.0 / (1.0 + jnp.exp(-x))


def _pick(n, pref):
    t = min(pref, n)
    while n % t:
        t //= 2
    return t


def _in_proj_kernel(x_ref, waq, wakT, wav, qpos, kpos, vones, wb, wc, wd, aq_o, akT_o, av_o, b_o, c_o, d_o):
    xb = x_ref[...].astype(BF16)
    aq_o[...] = (_dot(xb, waq[...]) * (A_DK ** -0.5) + qpos[...]).astype(BF16)
    akT_o[0] = (_dot_nt(wakT[...], xb) + kpos[...]).astype(BF16)
    av_o[0] = (_dot(xb, wav[...]) + vones[...]).astype(BF16)
    b_o[...] = _dot(xb, wb[...])
    c_o[...] = _dot(xb, wc[...])
    d_o[...] = _dot(xb, wd[...])


def in_proj(x2, B, S, w, tabs):
    T = B * S
    tm = _pick(S, 512)
    ns = S // tm
    aw = A_NP * A_QKW
    vw = A_HEADS * V_AUG_W
    full = lambda shape: pl.BlockSpec(shape, lambda i: (0,) * len(shape))
    rows = lambda wdt: pl.BlockSpec((tm, wdt), lambda i: (i, 0))
    return pl.pallas_call(
        _in_proj_kernel,
        grid=(T // tm,),
        in_specs=[rows(D_MODEL), full((D_MODEL, aw)), full((aw, D_MODEL)), full((D_MODEL, vw)),
                  pl.BlockSpec((tm, aw), lambda i: (i % ns, 0)),
                  pl.BlockSpec((aw, tm), lambda i: (0, i % ns)),
                  full((1, vw)),
                  full((D_MODEL, 5 * BR_W)), full((D_MODEL, C_PAD_W)), full((D_MODEL, 2 * BR_W))],
        out_specs=[rows(aw),
                   pl.BlockSpec((1, aw, tm), lambda i: (i // ns, 0, i % ns)),
                   pl.BlockSpec((1, tm, vw), lambda i: (i // ns, i % ns, 0)),
                   rows(5 * BR_W), rows(C_PAD_W), rows(2 * BR_W)],
        out_shape=[jax.ShapeDtypeStruct((T, aw), BF16),
                   jax.ShapeDtypeStruct((B, aw, S), BF16),
                   jax.ShapeDtypeStruct((B, S, vw), BF16),
                   jax.ShapeDtypeStruct((T, 5 * BR_W), F32),
                   jax.ShapeDtypeStruct((T, C_PAD_W), F32),
                   jax.ShapeDtypeStruct((T, 2 * BR_W), F32)],
        compiler_params=_cparams(("parallel",)),
        name="in_proj",
    )(x2, w["aq"], w["akT"], w["av"], tabs["a_qpos"], tabs["a_kpos"], tabs["v_ones"], w["b"], w["c"], w["d"])


def _softmax_tile(n_prob, qk, v_of, c_of, corr_of, m_s, acc_s):
    s_next = qk(0)
    for p in range(n_prob):
        s = s_next
        if p + 1 < n_prob:
            s_next = qk(p + 1)
        corr = corr_of(p)
        if corr is not None:
            s = s + corr
        c = c_of(p)
        m_old = m_s[p]
        m_new = jnp.maximum(m_old, jnp.max(s, axis=1, keepdims=True) + c)
        alpha = jnp.exp(m_old - m_new)
        pr = jnp.exp(s - (m_new - c))
        acc_s[p] = alpha * acc_s[p] + _dot(pr.astype(BF16), v_of(p))
        m_s[p] = m_new


def _diff_attn_kernel(lam_ref, q_ref, kT_ref, v_ref, g_ref, o_ref, qs, m_s, acc_s, *, tq, tk, nk, lambda_init):
    qi = pl.program_id(1)
    lane = lax.broadcasted_iota(jnp.int32, (1, A_NP * A_QKW), 1)
    is_pos = (lane % A_QKW) >= A_DK
    qa = q_ref[...].astype(F32)
    qneg = jnp.where(is_pos, -qa, qa)
    for p in range(A_NP):
        qs[p] = qa[:, A_QKW * p:A_QKW * (p + 1)].astype(BF16)
        qs[A_NP + p] = qneg[:, A_QKW * p:A_QKW * (p + 1)].astype(BF16)
    m_s[...] = jnp.full(m_s.shape, -jnp.inf, F32)
    acc_s[...] = jnp.zeros(acc_s.shape, F32)

    def tile(j, variant, straddle):
        j0 = pl.multiple_of(j * tk, tk)
        off = (qi * tq - j0).astype(F32)
        if straddle:
            d = (lax.broadcasted_iota(jnp.int32, (tq, tk), 0)
                 - lax.broadcasted_iota(jnp.int32, (tq, tk), 1)).astype(F32) + off
            dpos = jnp.maximum(d, 0.0)

        def qk(p):
            return _dot(qs[variant * A_NP + p], kT_ref[0, A_QKW * p:A_QKW * (p + 1), pl.ds(j0, tk)])

        _softmax_tile(
            A_NP, qk,
            v_of=lambda p: v_ref[0, pl.ds(j0, tk), V_AUG_W * (p // 2):V_AUG_W * (p // 2 + 1)],
            c_of=lambda p: (off if variant == 0 else -off) * ALIBI_SLOPES[p // 2],
            corr_of=lambda p: dpos * (-2.0 * ALIBI_SLOPES[p // 2]) if straddle else None,
            m_s=m_s, acc_s=acc_s)

    def left(j, carry):
        tile(j, 1, False)
        return carry

    def right(j, carry):
        tile(j, 0, False)
        return carry

    jd = qi // (tk // tq)
    lax.fori_loop(0, jd, left, 0)
    tile(jd, 0, True)
    lax.fori_loop(jd + 1, nk, right, 0)

    lam = lam_ref[0]
    for h in range(A_HEADS):
        a0 = acc_s[2 * h]
        a1 = acc_s[2 * h + 1]
        o = a0[:, :A_DV] / a0[:, A_DV:A_DV + 1] - lam * (a1[:, :A_DV] / a1[:, A_DV:A_DV + 1])
        ms = jnp.mean(o * o, axis=1, keepdims=True)
        o = o * lax.rsqrt(ms + RMS_EPS) * g_ref[...] * (1.0 - lambda_init)
        o_ref[:, A_DV * h:A_DV * (h + 1)] = o.astype(BF16)


def _att_tiles(S, tq_pref=ATT_TQ):
    tk = _pick(S, ATT_TK)
    tq = _pick(tk, tq_pref)
    return tq, tk


def diff_attention(aq, akT, av, lam, subln_g, layer_idx, B, S):
    tq, tk = _att_tiles(S)
    nq = S // tq
    aw = A_NP * A_QKW
    vw = A_HEADS * V_AUG_W
    lambda_init = 0.8 - 0.6 * math.exp(-0.3 * layer_idx)
    kern = functools.partial(_diff_attn_kernel, tq=tq, tk=tk, nk=S // tk, lambda_init=lambda_init)
    return pl.pallas_call(
        kern,
        grid=(B, nq),
        in_specs=[pl.BlockSpec(memory_space=pltpu.SMEM),
                  pl.BlockSpec((tq, aw), lambda b, i: (b * nq + i, 0)),
                  pl.BlockSpec((1, aw, S), lambda b, i: (b, 0, 0)),
                  pl.BlockSpec((1, S, vw), lambda b, i: (b, 0, 0)),
                  pl.BlockSpec((1, A_DV), lambda b, i: (0, 0))],
        out_specs=pl.BlockSpec((tq, BR_W), lambda b, i: (b * nq + i, 0)),
        out_shape=jax.ShapeDtypeStruct((B * S, BR_W), BF16),
        scratch_shapes=[pltpu.VMEM((2 * A_NP, tq, A_QKW), BF16),
                        pltpu.VMEM((A_NP, tq, 1), F32),
                        pltpu.VMEM((A_NP, tq, V_AUG_W), F32)],
        compiler_params=_cparams(("parallel", "arbitrary")),
        name="diff_attn",
    )(lam, aq, akT, av, subln_g)


def _mla_prep_kernel(c_ref, qg_ref, kvg_ref, wuq, wuq_rot, wukT, wuv, vones, cosq, sinq, cosk, sink, eye_ref,
                     q_o, kT_o, v_o):
    c = c_ref[...]
    cq = c[:, :C_Q_LORA]
    ckv = c[:, C_Q_LORA:C_Q_LORA + C_KV_LORA]
    ckr = c[:, C_Q_LORA + C_KV_LORA:C_Q_LORA + C_KV_LORA + C_ROPE]
    ckr_rot = c[:, C_Q_LORA + C_KV_LORA + C_ROPE:C_Q_LORA + C_KV_LORA + 2 * C_ROPE]
    nq = (cq * lax.rsqrt(jnp.mean(cq * cq, axis=1, keepdims=True) + RMS_EPS) * qg_ref[...]).astype(BF16)
    q = _dot(nq, wuq[...]) * cosq[...] + _dot(nq, wuq_rot[...]) * sinq[...]
    q_o[...] = (q * (C_QK ** -0.5)).astype(BF16)
    nkv = (ckv * lax.rsqrt(jnp.mean(ckv * ckv, axis=1, keepdims=True) + RMS_EPS) * kvg_ref[...]).astype(BF16)
    knT = _dot_nt(wukT[...], nkv).astype(BF16)
    v_o[0] = (_dot(nkv, wuv[...]) + vones[...]).astype(BF16)
    kr = (ckr * cosk[...] + ckr_rot * sink[...]).astype(BF16)
    krT = _dot_nt(eye_ref[...], kr).astype(BF16)
    for h in range(C_HEADS):
        kT_o[0, C_QK * h:C_QK * h + C_NOPE, :] = knT[C_NOPE * h:C_NOPE * (h + 1), :]
        kT_o[0, C_QK * h + C_NOPE:C_QK * (h + 1), :] = krT


def mla_prep(c_all, w, tabs, B, S):
    T = B * S
    tm = _pick(S, 512)
    ns = S // tm
    full = lambda shape: pl.BlockSpec(shape, lambda i: (0,) * len(shape))
    rows = lambda wdt: pl.BlockSpec((tm, wdt), lambda i: (i, 0))
    pos = lambda wdt: pl.BlockSpec((tm, wdt), lambda i: (i % ns, 0))
    qw = C_HEADS * C_QK
    vw = C_HEADS * V_AUG_W
    return pl.pallas_call(
        _mla_prep_kernel,
        grid=(T // tm,),
        in_specs=[rows(C_PAD_W), full((1, C_Q_LORA)), full((1, C_KV_LORA)), full((C_Q_LORA, qw)),
                  full((C_Q_LORA, qw)), full((C_HEADS * C_NOPE, C_KV_LORA)), full((C_KV_LORA, vw)),
                  full((1, vw)), pos(qw), pos(qw), pos(C_ROPE), pos(C_ROPE), full((C_ROPE, C_ROPE))],
        out_specs=[rows(qw),
                   pl.BlockSpec((1, qw, tm), lambda i: (i // ns, 0, i % ns)),
                   pl.BlockSpec((1, tm, vw), lambda i: (i // ns, i % ns, 0))],
        out_shape=[jax.ShapeDtypeStruct((T, qw), BF16),
                   jax.ShapeDtypeStruct((B, qw, S), BF16),
                   jax.ShapeDtypeStruct((B, S, vw), BF16)],
        compiler_params=_cparams(("parallel",)),
        name="mla_prep",
    )(c_all, w["q_norm_g"], w["kv_norm_g"], w["wuq"], w["wuq_rot"], w["wukT"], w["wuv"], tabs["v_ones"],
      tabs["cosq"], tabs["sinq"], tabs["cosk"], tabs["sink"], tabs["eye"])


def _mla_attn_kernel(q_ref, kT_ref, v_ref, o_ref, qs, m_s, acc_s, *, tk, nk):
    for h in range(C_HEADS):
        qs[h] = q_ref[:, C_QK * h:C_QK * (h + 1)]
    m_s[...] = jnp.full(m_s.shape, -jnp.inf, F32)
    acc_s[...] = jnp.zeros(acc_s.shape, F32)

    def tile(j, carry):
        j0 = pl.multiple_of(j * tk, tk)
        _softmax_tile(
            C_HEADS,
            qk=lambda h: _dot(qs[h], kT_ref[0, C_QK * h:C_QK * (h + 1), pl.ds(j0, tk)]),
            v_of=lambda h: v_ref[0, pl.ds(j0, tk), V_AUG_W * h:V_AUG_W * (h + 1)],
            c_of=lambda h: 0.0,
            corr_of=lambda h: None,
            m_s=m_s, acc_s=acc_s)
        return carry

    lax.fori_loop(0, nk, tile, 0)
    for h in range(C_HEADS):
        a = acc_s[h]
        o_ref[:, C_V * h:C_V * (h + 1)] = (a[:, :C_V] / a[:, C_V:C_V + 1]).astype(BF16)


def mla_attention(q, kT, v, B, S):
    tq, tk = _att_tiles(S, MLA_TQ)
    nq = S // tq
    qw = C_HEADS * C_QK
    vw = C_HEADS * V_AUG_W
    kern = functools.partial(_mla_attn_kernel, tk=tk, nk=S // tk)
    return pl.pallas_call(
        kern,
        grid=(B, nq),
        in_specs=[pl.BlockSpec((tq, qw), lambda b, i: (b * nq + i, 0)),
                  pl.BlockSpec((1, qw, S), lambda b, i: (b, 0, 0)),
                  pl.BlockSpec((1, S, vw), lambda b, i: (b, 0, 0))],
        out_specs=pl.BlockSpec((tq, BR_W), lambda b, i: (b * nq + i, 0)),
        out_shape=jax.ShapeDtypeStruct((B * S, BR_W), BF16),
        scratch_shapes=[pltpu.VMEM((C_HEADS, tq, C_QK), BF16),
                        pltpu.VMEM((C_HEADS, tq, 1), F32),
                        pltpu.VMEM((C_HEADS, tq, V_AUG_W), F32)],
        compiler_params=_cparams(("parallel", "arbitrary")),
        name="mla_attn",
    )(q, kT, v)


HGRN_LEVELS = (64, 32, 16, 8, 4, 2, 1)
N_HGRN_SUMS = len(HGRN_LEVELS) + 2


def _hgrn_constants():
    C = CHUNK
    t = np.arange(C)
    sums = np.zeros((2, N_HGRN_SUMS, C, C), np.float32)
    masks = np.zeros((2, len(HGRN_LEVELS) + 1, C, C), np.float32)
    for li, m in enumerate(HGRN_LEVELS):
        blk = t // m
        for tt in range(C):
            if blk[tt] % 2 == 1:
                sums[0, li, tt, blk[tt] * m:tt + 1] = 1.0
            else:
                sums[0, li, tt, tt + 1:(blk[tt] + 1) * m] = 1.0
        masks[0, li] = ((blk[:, None] % 2 == 1) & (blk[None, :] == blk[:, None] - 1)).astype(np.float32)
    masks[0, len(HGRN_LEVELS)] = np.eye(C, dtype=np.float32)
    sums[0, len(HGRN_LEVELS)] = (t[None, :] <= t[:, None]).astype(np.float32)
    sums[0, len(HGRN_LEVELS) + 1] = (t[None, :] > t[:, None]).astype(np.float32)
    sums[1] = sums[0][:, ::-1, ::-1]
    masks[1] = masks[0][:, ::-1, ::-1]
    return sums.reshape(2, N_HGRN_SUMS * C, C), np.concatenate([masks, masks], axis=3)


def _hgrn_kernel(qf_ref, ff_ref, vf_ref, qb_ref, fb_ref, vb_ref, lb_ref, sums_ref, mask_ref,
                 of_ref, ob_ref, st_ref):
    C = CHUNK
    nlev = len(HGRN_LEVELS)

    @pl.when(pl.program_id(1) == 0)
    def _():
        st_ref[...] = jnp.zeros(st_ref.shape, F32)

    lane = lax.broadcasted_iota(jnp.int32, (1, 2 * B_DK), 1)
    head_lanes = (lane < B_DK, lane >= B_DK)
    r_i = lax.broadcasted_iota(jnp.int32, (2 * B_DK, 2 * B_DK), 0)
    c_i = lax.broadcasted_iota(jnp.int32, (2 * B_DK, 2 * B_DK), 1)
    same_head = (r_i < B_DK) == (c_i < B_DK)

    for d, (q_ref, f_ref, v_ref, o_ref) in enumerate(((qf_ref, ff_ref, vf_ref, of_ref),
                                                      (qb_ref, fb_ref, vb_ref, ob_ref))):
        x = q_ref[...]
        q = x * _sigmoid(x)
        lb = lb_ref[d]
        f = lb + (1.0 - lb) * _sigmoid(f_ref[...])
        logf = jnp.log(f)
        k = 1.0 - f
        v = v_ref[...]
        hi = logf.astype(BF16)
        lo = (logf - hi.astype(F32)).astype(BF16)
        sums = sums_ref[d]
        E = _dot(sums, hi) + _dot(sums, lo)
        for p in range(B_HEADS // 2):
            sl = slice(2 * B_DK * p, 2 * B_DK * (p + 1))
            Qp, Kp, Vp = q[:, sl], k[:, sl], v[:, sl]

            def by_head(a):
                return jnp.concatenate([jnp.where(head_lanes[0], a, 0.0), jnp.where(head_lanes[1], a, 0.0)],
                                       axis=0).astype(BF16)

            A = jnp.zeros((C, 2 * C), F32)
            for l in range(nlev + 1):
                if l < nlev:
                    G = jnp.exp(E[C * l:C * (l + 1), sl])
                    QG = (Qp * G).astype(BF16)
                    KG = Kp * G
                else:
                    QG = Qp.astype(BF16)
                    KG = Kp
                A = A + mask_ref[d, l] * _dot_nt(QG, by_head(KG))
            o = _dot(A.astype(BF16), by_head(Vp))
            eb = jnp.exp(E[C * nlev:C * (nlev + 1), sl])
            st = st_ref[d, p]
            o = o + _dot_nt((Qp * eb).astype(BF16), st.astype(BF16))
            o_ref[:, sl] = o
            Kd = (Kp * jnp.exp(E[C * (nlev + 1):C * (nlev + 2), sl])).astype(BF16)
            row = C - 1 if d == 0 else 0
            g = eb[row:row + 1, :]
            upd = _dot_tn(Vp.astype(BF16), Kd)
            st_ref[d, p] = st * g + jnp.where(same_head, upd, 0.0)


def hgrn2(b_all, lb, B, S):
    T = B * S
    C = CHUNK
    nc = S // C
    sums, masks = _hgrn_constants()
    sums = jnp.asarray(sums, BF16)
    masks = jnp.asarray(masks, F32)
    fwd = lambda col: pl.BlockSpec((C, BR_W), lambda b, n: (b * nc + n, col))
    bwd = lambda col: pl.BlockSpec((C, BR_W), lambda b, n: (b * nc + nc - 1 - n, col))
    full = lambda shape: pl.BlockSpec(shape, lambda b, n: (0,) * len(shape))
    return pl.pallas_call(
        _hgrn_kernel,
        grid=(B, nc),
        in_specs=[fwd(0), fwd(1), fwd(3), bwd(0), bwd(2), bwd(3),
                  full((2, 1, BR_W)), full(sums.shape), full(masks.shape)],
        out_specs=[pl.BlockSpec((C, BR_W), lambda b, n: (b * nc + n, 0)),
                   pl.BlockSpec((C, BR_W), lambda b, n: (b * nc + nc - 1 - n, 0))],
        out_shape=[jax.ShapeDtypeStruct((T, BR_W), F32), jax.ShapeDtypeStruct((T, BR_W), F32)],
        scratch_shapes=[pltpu.VMEM((2, B_HEADS // 2, 2 * B_DK, 2 * B_DK), F32)],
        compiler_params=_cparams(("parallel", "arbitrary")),
        name="hgrn2",
    )(b_all, b_all, b_all, b_all, b_all, b_all, lb, sums, masks)


HALO = 8


def _rglru_kernel(xf_ref, xfp_ref, xfn_ref, xb_ref, xbp_ref, xbn_ref, cw_ref, cb_ref, wa_ref, ba_ref,
                  wx_ref, bx_ref, sp_ref, hf_ref, hb_ref, carry_ref, *, tc):
    n = pl.program_id(1)
    nc = pl.num_programs(1)

    @pl.when(n == 0)
    def _():
        carry_ref[...] = jnp.zeros(carry_ref.shape, F32)

    row = lax.broadcasted_iota(jnp.int32, (tc, 1), 0)
    for d, (x_ref, xp_ref, xn_ref, o_ref) in enumerate(((xf_ref, xfp_ref, xfn_ref, hf_ref),
                                                        (xb_ref, xbp_ref, xbn_ref, hb_ref))):
        cidx = n if d == 0 else nc - 1 - n
        prev = jnp.where(cidx > 0, xp_ref[...], 0.0)
        nxt = jnp.where(cidx < nc - 1, xn_ref[...], 0.0)
        ext = jnp.concatenate([prev, x_ref[...], nxt], axis=0)
        xc = cb_ref[...]
        for j in range(CONV_W):
            off = HALO - CONV_LEFT + j
            xc = xc + ext[off:off + tc, :] * cw_ref[j:j + 1, :]
        xcb = xc.astype(BF16)
        r = _sigmoid(_dot(xcb, wa_ref[d]) + ba_ref[d])
        i = _sigmoid(_dot(xcb, wx_ref[d]) + bx_ref[d])
        log_a = (-RG_C) * r * sp_ref[d]
        a = jnp.exp(log_a)
        u = jnp.sqrt(1.0 - jnp.exp(2.0 * log_a)) * (i * xc)
        sh = 1
        while sh < tc:
            if d == 0:
                valid = row >= sh
                amt = sh
            else:
                valid = row < tc - sh
                amt = tc - sh
            a_sh = jnp.where(valid, pltpu.roll(a, amt, 0), 1.0)
            u_sh = jnp.where(valid, pltpu.roll(u, amt, 0), 0.0)
            u = a * u_sh + u
            a = a * a_sh
            sh *= 2
        h = u + a * carry_ref[d]
        o_ref[...] = h
        last = tc - 1 if d == 0 else 0
        carry_ref[d] = h[last:last + 1, :]


def rglru(d_all, w, B, S):
    T = B * S
    tc = _pick(S, 512)
    nc = S // tc
    hb = tc // HALO
    nrow8 = T // HALO
    fc = lambda b, n: b * nc + n
    bc = lambda b, n: b * nc + nc - 1 - n
    cur = lambda f: pl.BlockSpec((tc, BR_W), lambda b, n: (f(b, n), 0))
    prv = lambda f: pl.BlockSpec((HALO, BR_W), lambda b, n: (jnp.maximum(f(b, n) * hb - 1, 0), 0))
    nxt = lambda f: pl.BlockSpec((HALO, BR_W), lambda b, n: (jnp.minimum((f(b, n) + 1) * hb, nrow8 - 1), 0))
    full = lambda shape: pl.BlockSpec(shape, lambda b, n: (0,) * len(shape))
    kern = functools.partial(_rglru_kernel, tc=tc)
    return pl.pallas_call(
        kern,
        grid=(B, nc),
        in_specs=[cur(fc), prv(fc), nxt(fc), cur(bc), prv(bc), nxt(bc),
                  full((CONV_W, BR_W)), full((1, BR_W)), full((2, BR_W, BR_W)), full((2, 1, BR_W)),
                  full((2, BR_W, BR_W)), full((2, 1, BR_W)), full((2, 1, BR_W))],
        out_specs=[cur(fc), cur(bc)],
        out_shape=[jax.ShapeDtypeStruct((T, BR_W), F32), jax.ShapeDtypeStruct((T, BR_W), F32)],
        scratch_shapes=[pltpu.VMEM((2, 1, BR_W), F32)],
        compiler_params=_cparams(("parallel", "arbitrary")),
        name="rglru",
    )(d_all, d_all, d_all, d_all, d_all, d_all, w["conv_w"], w["conv_b"], w["wa"], w["ba"], w["wx"],
      w["bx"], w["sp"])


def _layer_norm(z, g, b):
    mu = jnp.mean(z, axis=1, keepdims=True)
    zc = z - mu
    var = jnp.mean(zc * zc, axis=1, keepdims=True)
    return zc * lax.rsqrt(var + LN_EPS) * g + b


def _merge_kernel(x_ref, ya_ref, of_ref, ob_ref, bg_ref, yc_ref, hf_ref, hb_ref, dg_ref, wg_ref, wbr_ref,
                  wo_ref, lng_ref, lnb_ref, hg_ref, havg_ref, wr_ref, x1_ref, aff_ref):
    x = x_ref[...]
    xb = x.astype(BF16)
    o = of_ref[...] + ob_ref[...]
    ms = _dot_split(o * o, havg_ref[...])
    bg = bg_ref[...]
    yb = o * lax.rsqrt(ms + RMS_EPS) * hg_ref[...] * (bg * _sigmoid(bg))
    dg = dg_ref[...]
    gelu = 0.5 * dg * (1.0 + jnp.tanh(0.7978845608028654 * (dg + 0.044715 * dg * dg * dg)))
    yd = (hf_ref[...] + hb_ref[...]) * gelu
    ys = (ya_ref[...], yb.astype(BF16), yc_ref[...], yd.astype(BF16))
    mix = None
    for i in range(N_BRANCH):
        gate = _sigmoid(_dot(xb, wg_ref[:, D_MODEL * i:D_MODEL * (i + 1)]))
        term = gate * _dot(ys[i], wbr_ref[i])
        mix = term if mix is None else mix + term
    z = ALPHA * x + _dot(mix.astype(BF16), wo_ref[...])
    x1 = _layer_norm(z, lng_ref[...], lnb_ref[...])
    x1_ref[...] = x1
    x1h = x1.astype(BF16)
    x1l = (x1 - x1h.astype(F32)).astype(BF16)
    logits = _dot_nt(x1h, wr_ref[0]) + (_dot_nt(x1l, wr_ref[0]) + _dot_nt(x1h, wr_ref[1]))
    e = jnp.exp(logits - jnp.max(logits, axis=1, keepdims=True))
    aff_ref[...] = e / jnp.sum(e, axis=1, keepdims=True)


def merge(x2, ya, of, ob, b_all, yc, hf, hb, d_all, w):
    T = x2.shape[0]
    tm = _pick(T, 512)
    full = lambda shape: pl.BlockSpec(shape, lambda i: (0,) * len(shape))
    rows = lambda wdt: pl.BlockSpec((tm, wdt), lambda i: (i, 0))
    colblk = lambda c: pl.BlockSpec((tm, BR_W), lambda i: (i, c))
    return pl.pallas_call(
        _merge_kernel,
        grid=(T // tm,),
        in_specs=[rows(D_MODEL), rows(BR_W), rows(BR_W), rows(BR_W), colblk(4), rows(BR_W), rows(BR_W),
                  rows(BR_W), colblk(1), full((D_MODEL, N_BRANCH * D_MODEL)),
                  full((N_BRANCH, BR_W, D_MODEL)), full((D_MODEL, D_MODEL)), full((1, D_MODEL)),
                  full((1, D_MODEL)), full((1, BR_W)), full((BR_W, BR_W)), full((2, N_EXPERTS, D_MODEL))],
        out_specs=[rows(D_MODEL), rows(N_EXPERTS)],
        out_shape=[jax.ShapeDtypeStruct((T, D_MODEL), F32), jax.ShapeDtypeStruct((T, N_EXPERTS), F32)],
        compiler_params=_cparams(("parallel",)),
        name="merge",
    )(x2, ya, of, ob, b_all, yc, hf, hb, d_all, w["gate"], w["branch"], w["out"], w["ln_g"], w["ln_b"],
      w["hgrn_norm_g"], w["head_avg"], w["router"])


def _select_kernel(aff_ref, sel_ref, *, cap, n_tok):
    aff = aff_ref[...]
    v = lax.bitcast_convert_type(aff, jnp.int32)
    E, R, L = aff.shape

    def count(pred):
        c = jnp.sum(pred.astype(F32), axis=1)
        return jnp.sum(c, axis=1, keepdims=True)

    def value_step(i, thr):
        cand = thr | jnp.left_shift(jnp.int32(1), 30 - i)
        ok = count(v >= cand[:, :, None]) >= cap
        return jnp.where(ok, cand, thr)

    thr = lax.fori_loop(0, 31, value_step, jnp.zeros((E, 1), jnp.int32))
    thr3 = thr[:, :, None]
    gt = v > thr3
    eq = v == thr3
    need = cap - count(gt)
    tok = (lax.broadcasted_iota(jnp.int32, (E, R, L), 1) * L
           + lax.broadcasted_iota(jnp.int32, (E, R, L), 2))
    nbits = max(1, int(n_tok).bit_length())

    def index_step(i, bound):
        cand = bound | jnp.left_shift(jnp.int32(1), nbits - 1 - i)
        ok = count(eq & (tok < cand[:, :, None])) <= need
        return jnp.where(ok, cand, bound)

    bound = lax.fori_loop(0, nbits, index_step, jnp.zeros((E, 1), jnp.int32))
    sel = gt | (eq & (tok < bound[:, :, None]))
    sel_ref[...] = sel.astype(jnp.int32)


def select_tokens(affT3, cap, n_tok):
    E, R, L = affT3.shape
    kern = functools.partial(_select_kernel, cap=cap, n_tok=n_tok)
    return pl.pallas_call(
        kern,
        grid=(1,),
        in_specs=[pl.BlockSpec((E, R, L), lambda i: (0, 0, 0))],
        out_specs=pl.BlockSpec((E, R, L), lambda i: (0, 0, 0)),
        out_shape=jax.ShapeDtypeStruct((E, R, L), jnp.int32),
        compiler_params=_cparams(("arbitrary",)),
        name="select_tokens",
    )(affT3)


FFN_FCHUNK = 512


def _ffn_kernel(idx_hbm, x_hbm, g_ref, wg_ref, wu_ref, wd_ref, ye_ref, idx_smem, xbuf, idx_sem, row_sem,
                *, tm, n_blocks):
    pair = pl.program_id(0) * pl.num_programs(1) + pl.program_id(1)
    last = n_blocks - 1

    def idx_copy(b, s):
        return pltpu.make_async_copy(idx_hbm.at[b], idx_smem.at[s], idx_sem.at[s])

    def row_copy(src_row, s, j):
        return pltpu.make_async_copy(x_hbm.at[pl.ds(src_row, 1), :], xbuf.at[s, pl.ds(j, 1), :],
                                     row_sem.at[s])

    def wait_rows(s):
        pltpu.make_async_copy(x_hbm.at[pl.ds(0, tm), :], xbuf.at[s], row_sem.at[s]).wait()

    @pl.when(pair == 0)
    def _():
        first = idx_copy(0, 0)
        first.start()
        first.wait()

        def issue(j, carry):
            row_copy(idx_smem[0, j], 0, j).start()
            return carry

        lax.fori_loop(0, tm, issue, 0)
        idx_copy(jnp.minimum(1, last), 1).start()

    n_chunks = D_EXPERT // FFN_FCHUNK
    per_chunk = tm // n_chunks
    for slot in range(2):
        nslot = 1 - slot
        blk = 2 * pair + slot
        idx_copy(0, nslot).wait()
        wait_rows(slot)
        xb = xbuf[slot].astype(BF16)
        acc = jnp.zeros((tm, D_MODEL), F32)
        for c in range(n_chunks):
            for j in range(per_chunk * c, per_chunk * (c + 1)):
                row_copy(idx_smem[nslot, j], nslot, j).start()
            fs = slice(FFN_FCHUNK * c, FFN_FCHUNK * (c + 1))
            hg = _dot(xb, wg_ref[0, :, fs])
            hu = _dot(xb, wu_ref[0, :, fs])
            h = (hg * _sigmoid(hg) * hu).astype(BF16)
            acc = acc + _dot(h, wd_ref[0, fs, :])
        ye_ref[tm * slot:tm * (slot + 1), :] = acc * g_ref[tm * slot:tm * (slot + 1), :]
        idx_copy(jnp.minimum(blk + 2, last), slot).start()

    @pl.when(2 * pair + 1 == last)
    def _():
        wait_rows(0)
        idx_copy(0, 1).wait()


def expert_ffn(idx, x1, g, wg, wu, wd):
    E, cap = idx.shape
    tm = _pick(cap // 2, 512)
    R = cap // tm
    RP = R // 2
    kern = functools.partial(_ffn_kernel, tm=tm, n_blocks=E * R)
    return pl.pallas_call(
        kern,
        grid=(E, RP),
        in_specs=[pl.BlockSpec(memory_space=pl.ANY),
                  pl.BlockSpec(memory_space=pl.ANY),
                  pl.BlockSpec((2 * tm, 1), lambda e, r: (e * RP + r, 0)),
                  pl.BlockSpec((1, D_MODEL, D_EXPERT), lambda e, r: (e, 0, 0)),
                  pl.BlockSpec((1, D_MODEL, D_EXPERT), lambda e, r: (e, 0, 0)),
                  pl.BlockSpec((1, D_EXPERT, D_MODEL), lambda e, r: (e, 0, 0))],
        out_specs=pl.BlockSpec((2 * tm, D_MODEL), lambda e, r: (e * RP + r, 0)),
        out_shape=jax.ShapeDtypeStruct((E * cap, D_MODEL), F32),
        scratch_shapes=[pltpu.SMEM((2, tm), jnp.int32),
                        pltpu.VMEM((2, tm, D_MODEL), F32),
                        pltpu.SemaphoreType.DMA((2,)),
                        pltpu.SemaphoreType.DMA((2,))],
        compiler_params=pltpu.CompilerParams(dimension_semantics=("arbitrary", "arbitrary"),
                                             vmem_limit_bytes=VMEM_LIMIT, disable_bounds_checks=True),
        name="expert_ffn",
    )(idx.reshape(E * R, tm), x1, g.reshape(E * cap, 1), wg, wu, wd)


CMB_TT = 256
CMB_PIECE = 32
CMB_CHUNK = 8
SUBLANES = 8


def _combine_kernel(p_ref, ye_hbm, sel_ref, x_ref, tri_ref, g_ref, b_ref, o_ref,
                    buf, meta, nfill, pre_ref, oh_ref, acc_ref, sem, *, cap, n_rows, n_tiles, max_pieces):
    i = pl.program_id(0)
    slot = i % 2
    E = N_EXPERTS
    PC = CMB_PIECE

    def piece_copy(src_row, s, k):
        return pltpu.make_async_copy(ye_hbm.at[pl.ds(src_row, PC), :], buf.at[s, pl.ds(k * PC, PC), :],
                                     sem.at[s])

    def issue_tile(tile, s):
        def expert_body(e, fill):
            lo = p_ref[tile * E + e]
            s0 = e * cap + lo
            s1 = e * cap + p_ref[(tile + 1) * E + e]
            a0 = (s0 // SUBLANES) * SUBLANES

            def piece(q, fill):
                start = a0 + PC * q
                start_c = pl.multiple_of(jnp.minimum(start, n_rows - PC), SUBLANES)
                piece_copy(start_c, s, fill).start()
                meta[s, 0, fill] = e
                meta[s, 1, fill] = start_c - s0
                meta[s, 2, fill] = start - start_c
                return fill + 1

            n_pieces = jnp.where(s1 > s0, (s1 - a0 + PC - 1) // PC, 0)
            return lax.fori_loop(0, n_pieces, piece, fill)

        nfill[s] = lax.fori_loop(0, E, expert_body, 0)

    @pl.when(i == 0)
    def _():
        buf[...] = jnp.zeros(buf.shape, F32)

        def clear(k, carry):
            for s in range(2):
                for f in range(3):
                    meta[s, f, k] = 0
            return carry

        lax.fori_loop(0, max_pieces, clear, 0)
        issue_tile(0, 0)

    @pl.when(i + 1 < n_tiles)
    def _():
        issue_tile(i + 1, 1 - slot)

    fill = nfill[slot]

    def wait_piece(k, carry):
        piece_copy(0, slot, k).wait()
        return carry

    lax.fori_loop(0, fill, wait_piece, 0)

    selv = sel_ref[...]
    pre_ref[...] = _dot(selv.astype(BF16), tri_ref[...])
    acc_ref[...] = jnp.zeros(acc_ref.shape, F32)
    row = lax.broadcasted_iota(jnp.int32, (PC, 1), 0)

    def chunk(c, carry):
        for k in range(CMB_CHUNK):
            pidx = c * CMB_CHUNK + k
            live = pidx < fill
            e = jnp.where(live, meta[slot, 0, pidx], 0)
            rank = jnp.where((row >= meta[slot, 2, pidx]) & live, row + meta[slot, 1, pidx], -1).astype(F32)
            hit = (pre_ref[pl.ds(e, 1), :] == rank) & (sel_ref[pl.ds(e, 1), :] > 0.5)
            oh_ref[PC * k:PC * (k + 1), :] = jnp.where(hit, 1.0, 0.0).astype(BF16)
        rows = buf[slot, pl.ds(pl.multiple_of(c * (CMB_CHUNK * PC), CMB_CHUNK * PC), CMB_CHUNK * PC), :]
        hi = rows.astype(BF16)
        lo = (rows - hi.astype(F32)).astype(BF16)
        oh = oh_ref[...]
        acc_ref[...] += _dot_tn(oh, hi) + _dot_tn(oh, lo)
        return carry

    lax.fori_loop(0, (fill + CMB_CHUNK - 1) // CMB_CHUNK, chunk, 0)
    o_ref[...] = _layer_norm(ALPHA * x_ref[...] + acc_ref[...], g_ref[...], b_ref[...])


def combine_norm(ye, sel_f, tile_prefix, x1, g, b, cap):
    T = x1.shape[0]
    tt = _pick(T, CMB_TT)
    n_tiles = T // tt
    max_pieces = N_EXPERTS * (tt // CMB_PIECE + 1)
    max_pieces = ((max_pieces + CMB_CHUNK - 1) // CMB_CHUNK) * CMB_CHUNK
    tri = jnp.asarray(np.triu(np.ones((tt, tt), np.float32), 1), BF16)
    kern = functools.partial(_combine_kernel, cap=cap, n_rows=ye.shape[0], n_tiles=n_tiles,
                             max_pieces=max_pieces)
    grid_spec = pltpu.PrefetchScalarGridSpec(
        num_scalar_prefetch=1,
        grid=(n_tiles,),
        in_specs=[pl.BlockSpec(memory_space=pl.ANY),
                  pl.BlockSpec((N_EXPERTS, tt), lambda i, p: (0, i)),
                  pl.BlockSpec((tt, D_MODEL), lambda i, p: (i, 0)),
                  pl.BlockSpec((tt, tt), lambda i, p: (0, 0)),
                  pl.BlockSpec((1, D_MODEL), lambda i, p: (0, 0)),
                  pl.BlockSpec((1, D_MODEL), lambda i, p: (0, 0))],
        out_specs=pl.BlockSpec((tt, D_MODEL), lambda i, p: (i, 0)),
        scratch_shapes=[pltpu.VMEM((2, max_pieces * CMB_PIECE, D_MODEL), F32),
                        pltpu.SMEM((2, 3, max_pieces), jnp.int32),
                        pltpu.SMEM((2,), jnp.int32),
                        pltpu.VMEM((N_EXPERTS, tt), F32),
                        pltpu.VMEM((CMB_CHUNK * CMB_PIECE, tt), BF16),
                        pltpu.VMEM((tt, D_MODEL), F32),
                        pltpu.SemaphoreType.DMA((2,))])
    return pl.pallas_call(
        kern,
        grid_spec=grid_spec,
        out_shape=jax.ShapeDtypeStruct((T, D_MODEL), F32),
        compiler_params=pltpu.CompilerParams(dimension_semantics=("arbitrary",), vmem_limit_bytes=VMEM_LIMIT,
                                             disable_bounds_checks=True),
        name="combine_norm",
    )(tile_prefix, ye, sel_f, x1, tri, g, b)


def _rot_cols(w, start):
    half = C_ROPE // 2
    return jnp.concatenate([-w[:, start + half:start + C_ROPE], w[:, start:start + half]], axis=1)


def _block_diag(w):
    n, c, _ = w.shape
    out = jnp.zeros((n * c, n * c), w.dtype)
    for i in range(n):
        out = out.at[i * c:(i + 1) * c, i * c:(i + 1) * c].set(w[i])
    return out


def _prep_layer(l, p, lb_all):
    w_in = p["w_in"][l]
    o = 0
    aq = w_in[:, o:o + BR_W]; o += BR_W
    ak = w_in[:, o:o + BR_W]; o += BR_W
    av = w_in[:, o:o + BR_W]; o += BR_W
    wb = w_in[:, o:o + 5 * BR_W]; o += 5 * BR_W
    c0 = o
    wcq = w_in[:, o:o + C_Q_LORA]; o += C_Q_LORA
    wckv = w_in[:, o:o + C_KV_LORA]; o += C_KV_LORA
    wckr = w_in[:, o:o + C_ROPE]; o += C_ROPE
    wd = w_in[:, o:o + 2 * BR_W]; o += 2 * BR_W
    wgate = w_in[:, o:o + N_BRANCH * D_MODEL]
    used = C_Q_LORA + C_KV_LORA + 2 * C_ROPE
    wc = jnp.concatenate([wcq, wckv, wckr, _rot_cols(w_in, c0 + C_Q_LORA + C_KV_LORA),
                          jnp.zeros((D_MODEL, C_PAD_W - used), F32)], axis=1)
    padq = jnp.zeros((D_MODEL, A_QKW - A_DK), F32)
    padv = jnp.zeros((D_MODEL, V_AUG_W - A_DV), F32)
    aq_w = jnp.concatenate([t for p_ in range(A_NP) for t in (aq[:, A_DK * p_:A_DK * (p_ + 1)], padq)], axis=1)
    ak_w = jnp.concatenate([t for p_ in range(A_NP) for t in (ak[:, A_DK * p_:A_DK * (p_ + 1)], padq)], axis=1)
    av_w = jnp.concatenate([t for h in range(A_HEADS) for t in (av[:, A_DV * h:A_DV * (h + 1)], padv)], axis=1)
    proj = dict(aq=aq_w.astype(BF16), akT=ak_w.T.astype(BF16), av=av_w.astype(BF16), b=wb.astype(BF16),
                c=wc.astype(BF16), d=wd.astype(BF16))

    wuq = p["mla_w_uq"][l]
    wuq_rot = jnp.zeros_like(wuq)
    for h in range(C_HEADS):
        r0 = C_QK * h + C_NOPE
        wuq_rot = wuq_rot.at[:, r0:r0 + C_ROPE].set(_rot_cols(wuq, r0))
    wukv = p["mla_w_ukv"][l].reshape(C_KV_LORA, C_HEADS, C_NOPE + C_V)
    wuv = jnp.concatenate([wukv[:, :, C_NOPE:], jnp.zeros((C_KV_LORA, C_HEADS, V_AUG_W - C_V), F32)], axis=2)
    mla = dict(q_norm_g=p["mla_q_norm_g"][l][None, :], kv_norm_g=p["mla_kv_norm_g"][l][None, :],
               wuq=wuq.astype(BF16), wuq_rot=wuq_rot.astype(BF16),
               wukT=wukv[:, :, :C_NOPE].reshape(C_KV_LORA, C_HEADS * C_NOPE).T.astype(BF16),
               wuv=wuv.reshape(C_KV_LORA, C_HEADS * V_AUG_W).astype(BF16))

    rg = dict(conv_w=p["rg_conv_w"][l], conv_b=p["rg_conv_b"][l][None, :],
              wa=jnp.stack([_block_diag(p["rg_w_a"][l, d]) for d in range(2)]).astype(BF16),
              ba=p["rg_b_a"][l][:, None, :],
              wx=jnp.stack([_block_diag(p["rg_w_x"][l, d]) for d in range(2)]).astype(BF16),
              bx=p["rg_b_x"][l][:, None, :],
              sp=jax.nn.softplus(-p["rg_lambda"][l].astype(F32))[:, None, :])

    head = np.arange(BR_W) // B_DK
    wr_hi = p["w_router"][l].astype(BF16)
    mrg = dict(gate=wgate.astype(BF16), branch=p["w_branch"][l].astype(BF16), out=p["w_out"][l].astype(BF16),
               ln_g=p["ln_g"][l, 0][None, :], ln_b=p["ln_b"][l, 0][None, :],
               hgrn_norm_g=p["hgrn_norm_g"][l][None, :],
               head_avg=jnp.asarray((head[:, None] == head[None, :]).astype(np.float32) / B_DK, BF16),
               router=jnp.stack([wr_hi.T, (p["w_router"][l] - wr_hi.astype(F32)).astype(BF16).T]))

    lp = p["diff_lambda"][l].astype(F32)
    lambda_init = 0.8 - 0.6 * math.exp(-0.3 * l)
    lam = (jnp.exp(jnp.sum(lp[0] * lp[1])) - jnp.exp(jnp.sum(lp[2] * lp[3])) + lambda_init).reshape(1)
    return dict(proj=proj, mla=mla, rg=rg, mrg=mrg, lam=lam, subln_g=p["diff_subln_g"][l][None, :],
                lb=lb_all[l][:, None, :],
                ffn=(p["w_e_gate"][l].astype(BF16), p["w_e_up"][l].astype(BF16), p["w_e_down"][l].astype(BF16)),
                ln2_g=p["ln_g"][l, 1][None, :], ln2_b=p["ln_b"][l, 1][None, :])


def _rope_tables(S):
    half = C_ROPE // 2
    inv = ROPE_BASE ** (-jnp.arange(0, C_ROPE, 2, dtype=F32) / C_ROPE)
    ang = jnp.arange(S, dtype=F32)[:, None] * inv[None, :]
    cos, sin = jnp.cos(ang), jnp.sin(ang)
    cosk = jnp.concatenate([cos, cos], axis=1)
    sink = jnp.concatenate([sin, sin], axis=1)
    ones = jnp.ones((S, C_NOPE), F32)
    zeros = jnp.zeros((S, C_NOPE), F32)
    cosq = jnp.concatenate([jnp.concatenate([ones, cosk], axis=1)] * C_HEADS, axis=1)
    sinq = jnp.concatenate([jnp.concatenate([zeros, sink], axis=1)] * C_HEADS, axis=1)
    tq, tk = _att_tiles(S)
    pos = jnp.arange(S, dtype=jnp.int32)
    i_rel = (pos % tq).astype(F32)
    j_rel = pos % tk
    j_hi = (16 * (j_rel // 16)).astype(F32)
    j_lo = (j_rel % 16).astype(F32)
    one = jnp.ones((S,), F32)
    qcols = jnp.stack([i_rel, one, one], axis=1)
    qpad = jnp.zeros((S, A_QKW - A_DK - 3), F32)
    qblock = jnp.concatenate([jnp.zeros((S, A_DK), F32), qcols, qpad], axis=1)
    a_qpos = jnp.concatenate([qblock] * A_NP, axis=1)
    kblocks = []
    for p_ in range(A_NP):
        sl = ALIBI_SLOPES[p_ // 2]
        krows = jnp.stack([sl * one, -sl * j_hi, -sl * j_lo], axis=0)
        kblocks += [jnp.zeros((A_DK, S), F32), krows, jnp.zeros((A_QKW - A_DK - 3, S), F32)]
    a_kpos = jnp.concatenate(kblocks, axis=0)
    v_ones = jnp.zeros((1, A_HEADS * V_AUG_W), F32).at[0, A_DV::V_AUG_W].set(1.0)
    return dict(cosq=cosq, sinq=sinq, cosk=cosk, sink=sink, eye=jnp.eye(C_ROPE, dtype=BF16),
                a_qpos=a_qpos, a_kpos=a_kpos, v_ones=v_ones)


def _layer(x2, B, S, l, w, tabs):
    T = B * S
    aq, akT, av, b_all, c_all, d_all = in_proj(x2, B, S, w["proj"], tabs)
    ya = diff_attention(aq, akT, av, w["lam"], w["subln_g"], l, B, S)
    of, ob = hgrn2(b_all, w["lb"], B, S)
    cq, ckT, cv = mla_prep(c_all, w["mla"], tabs, B, S)
    yc = mla_attention(cq, ckT, cv, B, S)
    hf, hb = rglru(d_all, w["rg"], B, S)
    x1, aff = merge(x2, ya, of, ob, b_all, yc, hf, hb, d_all, w["mrg"])

    cap = CAPACITY_FACTOR * T // N_EXPERTS
    affT = aff.T
    sel = select_tokens(affT.reshape(N_EXPERTS, T // 128, 128), cap, T).reshape(N_EXPERTS, T)
    tok = lax.broadcasted_iota(jnp.int32, sel.shape, 1)
    order = jnp.sort(jnp.where(sel > 0, tok, tok + T), axis=1)[:, :cap]
    g = jnp.take_along_axis(affT, order, axis=1)
    ye = expert_ffn(order, x1, g, *w["ffn"])
    tt = _pick(T, CMB_TT)
    counts = jnp.sum(sel.reshape(N_EXPERTS, T // tt, tt), axis=2)
    tile_prefix = jnp.concatenate([jnp.zeros((N_EXPERTS, 1), jnp.int32), jnp.cumsum(counts, axis=1)], axis=1)
    return combine_norm(ye, sel.astype(F32), tile_prefix.T.reshape(-1).astype(jnp.int32), x1,
                        w["ln2_g"], w["ln2_b"], cap)


def _trunk(x, weights):
    B, S, _ = x.shape
    tabs = _rope_tables(S)
    x2 = x.reshape(B * S, D_MODEL)
    for l in range(DEPTH):
        x2 = _layer(x2, B, S, l, weights[l], tabs)
    return x2.reshape(B, S, D_MODEL)


def kernel(x_prompt, x_sample, w_in, diff_lambda, diff_subln_g, hgrn_lb_logits, hgrn_norm_g, mla_q_norm_g,
           mla_w_uq, mla_kv_norm_g, mla_w_ukv, rg_conv_w, rg_conv_b, rg_w_a, rg_b_a, rg_w_x, rg_b_x, rg_lambda,
           w_branch, w_out, ln_g, ln_b, w_router, w_e_gate, w_e_up, w_e_down):
    lb_all = jnp.cumsum(jax.nn.softmax(hgrn_lb_logits.astype(F32), axis=0), axis=0)
    lb_all = lb_all - lb_all[:1]
    p = dict(w_in=w_in, diff_lambda=diff_lambda, diff_subln_g=diff_subln_g, hgrn_norm_g=hgrn_norm_g,
             mla_q_norm_g=mla_q_norm_g, mla_w_uq=mla_w_uq, mla_kv_norm_g=mla_kv_norm_g, mla_w_ukv=mla_w_ukv,
             rg_conv_w=rg_conv_w, rg_conv_b=rg_conv_b, rg_w_a=rg_w_a, rg_b_a=rg_b_a, rg_w_x=rg_w_x,
             rg_b_x=rg_b_x, rg_lambda=rg_lambda, w_branch=w_branch, w_out=w_out, ln_g=ln_g, ln_b=ln_b,
             w_router=w_router, w_e_gate=w_e_gate, w_e_up=w_e_up, w_e_down=w_e_down)
    weights = [_prep_layer(l, p, lb_all) for l in range(DEPTH)]
    return (_trunk(x_prompt, weights), _trunk(x_sample, weights))
```

```python
import functools
import math

import numpy as np
import jax
import jax.numpy as jnp
from jax import lax
from jax.experimental import pallas as pl
from jax.experimental.pallas import tpu as pltpu

F32 = jnp.float32
BF16 = jnp.bfloat16

D_MODEL = 1024
DEPTH = 2
N_BRANCH = 4
BR_W = 256
A_HEADS = 4
A_DK = 32
A_DV = 64
B_HEADS = 4
B_DK = 64
C_HEADS = 4
C_NOPE = 32
C_ROPE = 16
C_QK = C_NOPE + C_ROPE
C_V = 64
C_Q_LORA = 192
C_KV_LORA = 128
C_PAD_W = 384
ROPE_BASE = 10000.0
D_BLOCKS = 4
CONV_W = 4
CONV_LEFT = 2
RG_C = 8.0
N_EXPERTS = 16
D_EXPERT = 2048
CAPACITY_FACTOR = 2
CHUNK = 128
LN_EPS = 1e-5
RMS_EPS = 1e-6
ALPHA = (2 * DEPTH) ** 0.25
ALIBI_SLOPES = tuple(2.0 ** (-8.0 * (h + 1) / A_HEADS) for h in range(A_HEADS))
A_NP = 2 * A_HEADS
A_QKW = 48
V_AUG_W = 128
ATT_TQ = 256
MLA_TQ = 512
ATT_TK = 2048

VMEM_LIMIT = 56 * 1024 * 1024

NT_DIMS = (((1,), (1,)), ((), ()))
TN_DIMS = (((0,), (0,)), ((), ()))


def _cparams(sem):
    return pltpu.CompilerParams(dimension_semantics=sem, vmem_limit_bytes=VMEM_LIMIT)


def _dot(a, b):
    return jnp.dot(a, b, preferred_element_type=F32)


def _dot_nt(a, b):
    return lax.dot_general(a, b, NT_DIMS, preferred_element_type=F32)


def _dot_tn(a, b):
    return lax.dot_general(a, b, TN_DIMS, preferred_element_type=F32)


def _dot_split(a, b_bf16):
    hi = a.astype(BF16)
    lo = (a - hi.astype(F32)).astype(BF16)
    return _dot(hi, b_bf16) + _dot(lo, b_bf16)


def _sigmoid(x):
    return 1.0 / (1.0 + jnp.exp(-x))


def _pick(n, pref):
    t = min(pref, n)
    while n % t:
        t //= 2
    return t


def _in_proj_kernel(x_ref, waq, wakT, wav, qpos, kpos, vones, wb, wc, wd, aq_o, akT_o, av_o, b_o, c_o, d_o):
    xb = x_ref[...].astype(BF16)
    aq_o[...] = (_dot(xb, waq[...]) * (A_DK ** -0.5) + qpos[...]).astype(BF16)
    akT_o[0] = (_dot_nt(wakT[...], xb) + kpos[...]).astype(BF16)
    av_o[0] = (_dot(xb, wav[...]) + vones[...]).astype(BF16)
    b_o[...] = _dot(xb, wb[...])
    c_o[...] = _dot(xb, wc[...])
    d_o[...] = _dot(xb, wd[...])


def in_proj(x2, B, S, w, tabs):
    T = B * S
    tm = _pick(S, 512)
    ns = S // tm
    aw = A_NP * A_QKW
    vw = A_HEADS * V_AUG_W
    full = lambda shape: pl.BlockSpec(shape, lambda i: (0,) * len(shape))
    rows = lambda wdt: pl.BlockSpec((tm, wdt), lambda i: (i, 0))
    return pl.pallas_call(
        _in_proj_kernel,
        grid=(T // tm,),
        in_specs=[rows(D_MODEL), full((D_MODEL, aw)), full((aw, D_MODEL)), full((D_MODEL, vw)),
                  pl.BlockSpec((tm, aw), lambda i: (i % ns, 0)),
                  pl.BlockSpec((aw, tm), lambda i: (0, i % ns)),
                  full((1, vw)),
                  full((D_MODEL, 5 * BR_W)), full((D_MODEL, C_PAD_W)), full((D_MODEL, 2 * BR_W))],
        out_specs=[rows(aw),
                   pl.BlockSpec((1, aw, tm), lambda i: (i // ns, 0, i % ns)),
                   pl.BlockSpec((1, tm, vw), lambda i: (i // ns, i % ns, 0)),
                   rows(5 * BR_W), rows(C_PAD_W), rows(2 * BR_W)],
        out_shape=[jax.ShapeDtypeStruct((T, aw), BF16),
                   jax.ShapeDtypeStruct((B, aw, S), BF16),
                   jax.ShapeDtypeStruct((B, S, vw), BF16),
                   jax.ShapeDtypeStruct((T, 5 * BR_W), F32),
                   jax.ShapeDtypeStruct((T, C_PAD_W), F32),
                   jax.ShapeDtypeStruct((T, 2 * BR_W), F32)],
        compiler_params=_cparams(("parallel",)),
        name="in_proj",
    )(x2, w["aq"], w["akT"], w["av"], tabs["a_qpos"], tabs["a_kpos"], tabs["v_ones"], w["b"], w["c"], w["d"])


def _softmax_tile(n_prob, qk, v_of, c_of, corr_of, m_s, acc_s):
    s_next = qk(0)
    for p in range(n_prob):
        s = s_next
        if p + 1 < n_prob:
            s_next = qk(p + 1)
        corr = corr_of(p)
        if corr is not None:
            s = s + corr
        c = c_of(p)
        m_old = m_s[p]
        m_new = jnp.maximum(m_old, jnp.max(s, axis=1, keepdims=True) + c)
        alpha = jnp.exp(m_old - m_new)
        pr = jnp.exp(s - (m_new - c))
        acc_s[p] = alpha * acc_s[p] + _dot(pr.astype(BF16), v_of(p))
        m_s[p] = m_new


def _diff_attn_kernel(lam_ref, q_ref, kT_ref, v_ref, g_ref, o_ref, qs, m_s, acc_s, *, tq, tk, nk, lambda_init):
    qi = pl.program_id(1)
    lane = lax.broadcasted_iota(jnp.int32, (1, A_NP * A_QKW), 1)
    is_pos = (lane % A_QKW) >= A_DK
    qa = q_ref[...].astype(F32)
    qneg = jnp.where(is_pos, -qa, qa)
    for p in range(A_NP):
        qs[p] = qa[:, A_QKW * p:A_QKW * (p + 1)].astype(BF16)
        qs[A_NP + p] = qneg[:, A_QKW * p:A_QKW * (p + 1)].astype(BF16)
    m_s[...] = jnp.full(m_s.shape, -jnp.inf, F32)
    acc_s[...] = jnp.zeros(acc_s.shape, F32)

    def tile(j, variant, straddle):
        j0 = pl.multiple_of(j * tk, tk)
        off = (qi * tq - j0).astype(F32)
        if straddle:
            d = (lax.broadcasted_iota(jnp.int32, (tq, tk), 0)
                 - lax.broadcasted_iota(jnp.int32, (tq, tk), 1)).astype(F32) + off
            dpos = jnp.maximum(d, 0.0)

        def qk(p):
            return _dot(qs[variant * A_NP + p], kT_ref[0, A_QKW * p:A_QKW * (p + 1), pl.ds(j0, tk)])

        _softmax_tile(
            A_NP, qk,
            v_of=lambda p: v_ref[0, pl.ds(j0, tk), V_AUG_W * (p // 2):V_AUG_W * (p // 2 + 1)],
            c_of=lambda p: (off if variant == 0 else -off) * ALIBI_SLOPES[p // 2],
            corr_of=lambda p: dpos * (-2.0 * ALIBI_SLOPES[p // 2]) if straddle else None,
            m_s=m_s, acc_s=acc_s)

    def left(j, carry):
        tile(j, 1, False)
        return carry

    def right(j, carry):
        tile(j, 0, False)
        return carry

    jd = qi // (tk // tq)
    lax.fori_loop(0, jd, left, 0)
    tile(jd, 0, True)
    lax.fori_loop(jd + 1, nk, right, 0)

    lam = lam_ref[0]
    for h in range(A_HEADS):
        a0 = acc_s[2 * h]
        a1 = acc_s[2 * h + 1]
        o = a0[:, :A_DV] / a0[:, A_DV:A_DV + 1] - lam * (a1[:, :A_DV] / a1[:, A_DV:A_DV + 1])
        ms = jnp.mean(o * o, axis=1, keepdims=True)
        o = o * lax.rsqrt(ms + RMS_EPS) * g_ref[...] * (1.0 - lambda_init)
        o_ref[:, A_DV * h:A_DV * (h + 1)] = o.astype(BF16)


def _att_tiles(S, tq_pref=ATT_TQ):
    tk = _pick(S, ATT_TK)
    tq = _pick(tk, tq_pref)
    return tq, tk


def diff_attention(aq, akT, av, lam, subln_g, layer_idx, B, S):
    tq, tk = _att_tiles(S)
    nq = S // tq
    aw = A_NP * A_QKW
    vw = A_HEADS * V_AUG_W
    lambda_init = 0.8 - 0.6 * math.exp(-0.3 * layer_idx)
    kern = functools.partial(_diff_attn_kernel, tq=tq, tk=tk, nk=S // tk, lambda_init=lambda_init)
    return pl.pallas_call(
        kern,
        grid=(B, nq),
        in_specs=[pl.BlockSpec(memory_space=pltpu.SMEM),
                  pl.BlockSpec((tq, aw), lambda b, i: (b * nq + i, 0)),
                  pl.BlockSpec((1, aw, S), lambda b, i: (b, 0, 0)),
                  pl.BlockSpec((1, S, vw), lambda b, i: (b, 0, 0)),
                  pl.BlockSpec((1, A_DV), lambda b, i: (0, 0))],
        out_specs=pl.BlockSpec((tq, BR_W), lambda b, i: (b * nq + i, 0)),
        out_shape=jax.ShapeDtypeStruct((B * S, BR_W), BF16),
        scratch_shapes=[pltpu.VMEM((2 * A_NP, tq, A_QKW), BF16),
                        pltpu.VMEM((A_NP, tq, 1), F32),
                        pltpu.VMEM((A_NP, tq, V_AUG_W), F32)],
        compiler_params=_cparams(("parallel", "arbitrary")),
        name="diff_attn",
    )(lam, aq, akT, av, subln_g)


def _mla_prep_kernel(c_ref, qg_ref, kvg_ref, wuq, wuq_rot, wukT, wuv, vones, cosq, sinq, cosk, sink, eye_ref,
                     q_o, kT_o, v_o):
    c = c_ref[...]
    cq = c[:, :C_Q_LORA]
    ckv = c[:, C_Q_LORA:C_Q_LORA + C_KV_LORA]
    ckr = c[:, C_Q_LORA + C_KV_LORA:C_Q_LORA + C_KV_LORA + C_ROPE]
    ckr_rot = c[:, C_Q_LORA + C_KV_LORA + C_ROPE:C_Q_LORA + C_KV_LORA + 2 * C_ROPE]
    nq = (cq * lax.rsqrt(jnp.mean(cq * cq, axis=1, keepdims=True) + RMS_EPS) * qg_ref[...]).astype(BF16)
    q = _dot(nq, wuq[...]) * cosq[...] + _dot(nq, wuq_rot[...]) * sinq[...]
    q_o[...] = (q * (C_QK ** -0.5)).astype(BF16)
    nkv = (ckv * lax.rsqrt(jnp.mean(ckv * ckv, axis=1, keepdims=True) + RMS_EPS) * kvg_ref[...]).astype(BF16)
    knT = _dot_nt(wukT[...], nkv).astype(BF16)
    v_o[0] = (_dot(nkv, wuv[...]) + vones[...]).astype(BF16)
    kr = (ckr * cosk[...] + ckr_rot * sink[...]).astype(BF16)
    krT = _dot_nt(eye_ref[...], kr).astype(BF16)
    for h in range(C_HEADS):
        kT_o[0, C_QK * h:C_QK * h + C_NOPE, :] = knT[C_NOPE * h:C_NOPE * (h + 1), :]
        kT_o[0, C_QK * h + C_NOPE:C_QK * (h + 1), :] = krT


def mla_prep(c_all, w, tabs, B, S):
    T = B * S
    tm = _pick(S, 512)
    ns = S // tm
    full = lambda shape: pl.BlockSpec(shape, lambda i: (0,) * len(shape))
    rows = lambda wdt: pl.BlockSpec((tm, wdt), lambda i: (i, 0))
    pos = lambda wdt: pl.BlockSpec((tm, wdt), lambda i: (i % ns, 0))
    qw = C_HEADS * C_QK
    vw = C_HEADS * V_AUG_W
    return pl.pallas_call(
        _mla_prep_kernel,
        grid=(T // tm,),
        in_specs=[rows(C_PAD_W), full((1, C_Q_LORA)), full((1, C_KV_LORA)), full((C_Q_LORA, qw)),
                  full((C_Q_LORA, qw)), full((C_HEADS * C_NOPE, C_KV_LORA)), full((C_KV_LORA, vw)),
                  full((1, vw)), pos(qw), pos(qw), pos(C_ROPE), pos(C_ROPE), full((C_ROPE, C_ROPE))],
        out_specs=[rows(qw),
                   pl.BlockSpec((1, qw, tm), lambda i: (i // ns, 0, i % ns)),
                   pl.BlockSpec((1, tm, vw), lambda i: (i // ns, i % ns, 0))],
        out_shape=[jax.ShapeDtypeStruct((T, qw), BF16),
                   jax.ShapeDtypeStruct((B, qw, S), BF16),
                   jax.ShapeDtypeStruct((B, S, vw), BF16)],
        compiler_params=_cparams(("parallel",)),
        name="mla_prep",
    )(c_all, w["q_norm_g"], w["kv_norm_g"], w["wuq"], w["wuq_rot"], w["wukT"], w["wuv"], tabs["v_ones"],
      tabs["cosq"], tabs["sinq"], tabs["cosk"], tabs["sink"], tabs["eye"])


def _mla_attn_kernel(q_ref, kT_ref, v_ref, o_ref, qs, m_s, acc_s, *, tk, nk):
    for h in range(C_HEADS):
        qs[h] = q_ref[:, C_QK * h:C_QK * (h + 1)]
    m_s[...] = jnp.full(m_s.shape, -jnp.inf, F32)
    acc_s[...] = jnp.zeros(acc_s.shape, F32)

    def tile(j, carry):
        j0 = pl.multiple_of(j * tk, tk)
        _softmax_tile(
            C_HEADS,
            qk=lambda h: _dot(qs[h], kT_ref[0, C_QK * h:C_QK * (h + 1), pl.ds(j0, tk)]),
            v_of=lambda h: v_ref[0, pl.ds(j0, tk), V_AUG_W * h:V_AUG_W * (h + 1)],
            c_of=lambda h: 0.0,
            corr_of=lambda h: None,
            m_s=m_s, acc_s=acc_s)
        return carry

    lax.fori_loop(0, nk, tile, 0)
    for h in range(C_HEADS):
        a = acc_s[h]
        o_ref[:, C_V * h:C_V * (h + 1)] = (a[:, :C_V] / a[:, C_V:C_V + 1]).astype(BF16)


def mla_attention(q, kT, v, B, S):
    tq, tk = _att_tiles(S, MLA_TQ)
    nq = S // tq
    qw = C_HEADS * C_QK
    vw = C_HEADS * V_AUG_W
    kern = functools.partial(_mla_attn_kernel, tk=tk, nk=S // tk)
    return pl.pallas_call(
        kern,
        grid=(B, nq),
        in_specs=[pl.BlockSpec((tq, qw), lambda b, i: (b * nq + i, 0)),
                  pl.BlockSpec((1, qw, S), lambda b, i: (b, 0, 0)),
                  pl.BlockSpec((1, S, vw), lambda b, i: (b, 0, 0))],
        out_specs=pl.BlockSpec((tq, BR_W), lambda b, i: (b * nq + i, 0)),
        out_shape=jax.ShapeDtypeStruct((B * S, BR_W), BF16),
        scratch_shapes=[pltpu.VMEM((C_HEADS, tq, C_QK), BF16),
                        pltpu.VMEM((C_HEADS, tq, 1), F32),
                        pltpu.VMEM((C_HEADS, tq, V_AUG_W), F32)],
        compiler_params=_cparams(("parallel", "arbitrary")),
        name="mla_attn",
    )(q, kT, v)


HGRN_LEVELS = (64, 32, 16, 8, 4, 2, 1)
N_HGRN_SUMS = len(HGRN_LEVELS) + 1


def _hgrn_constants():
    C = CHUNK
    t = np.arange(C)
    sums = np.zeros((2, len(HGRN_LEVELS) + 2, C, C), np.float32)
    masks = np.zeros((2, len(HGRN_LEVELS) + 1, C, C), np.float32)
    for li, m in enumerate(HGRN_LEVELS):
        blk = t // m
        for tt in range(C):
            if blk[tt] % 2 == 1:
                sums[0, li, tt, blk[tt] * m:tt + 1] = 1.0
            else:
                sums[0, li, tt, tt + 1:(blk[tt] + 1) * m] = 1.0
        masks[0, li] = ((blk[:, None] % 2 == 1) & (blk[None, :] == blk[:, None] - 1)).astype(np.float32)
    masks[0, len(HGRN_LEVELS)] = np.eye(C, dtype=np.float32)
    sums[0, len(HGRN_LEVELS)] = (t[None, :] <= t[:, None]).astype(np.float32)
    sums[0, len(HGRN_LEVELS) + 1] = (t[None, :] > t[:, None]).astype(np.float32)
    sums[1] = sums[0][:, ::-1, ::-1]
    masks[1] = masks[0][:, ::-1, ::-1]
    keep = [i for i in range(len(HGRN_LEVELS) + 2) if i != len(HGRN_LEVELS) - 1]
    sums = sums[:, keep]
    return sums.reshape(2, N_HGRN_SUMS * C, C), np.concatenate([masks, masks], axis=3)


def _hgrn_kernel(qf_ref, ff_ref, vf_ref, qb_ref, fb_ref, vb_ref, lb_ref, sums_ref, mask_ref,
                 of_ref, ob_ref, st_ref):
    C = CHUNK
    nlev = len(HGRN_LEVELS)

    @pl.when(pl.program_id(1) == 0)
    def _():
        st_ref[...] = jnp.zeros(st_ref.shape, F32)

    lane = lax.broadcasted_iota(jnp.int32, (1, 2 * B_DK), 1)
    head_lanes = (lane < B_DK, lane >= B_DK)
    r_i = lax.broadcasted_iota(jnp.int32, (2 * B_DK, 2 * B_DK), 0)
    c_i = lax.broadcasted_iota(jnp.int32, (2 * B_DK, 2 * B_DK), 1)
    same_head = (r_i < B_DK) == (c_i < B_DK)
    odd_row = (lax.broadcasted_iota(jnp.int32, (C, 1), 0) % 2) == 1

    for d, (q_ref, f_ref, v_ref, o_ref) in enumerate(((qf_ref, ff_ref, vf_ref, of_ref),
                                                      (qb_ref, fb_ref, vb_ref, ob_ref))):
        x = q_ref[...]
        q = x * _sigmoid(x)
        lb = lb_ref[d]
        f = lb + (1.0 - lb) * _sigmoid(f_ref[...])
        logf = jnp.log(f)
        k = 1.0 - f
        v = v_ref[...]
        hi = logf.astype(BF16)
        lo = (logf - hi.astype(F32)).astype(BF16)
        sums = sums_ref[d]
        E = _dot(sums, hi) + _dot(sums, lo)
        for p in range(B_HEADS // 2):
            sl = slice(2 * B_DK * p, 2 * B_DK * (p + 1))
            Qp, Kp, Vp = q[:, sl], k[:, sl], v[:, sl]

            def by_head(a):
                return jnp.concatenate([jnp.where(head_lanes[0], a, 0.0), jnp.where(head_lanes[1], a, 0.0)],
                                       axis=0).astype(BF16)

            A = jnp.zeros((C, 2 * C), F32)
            for l in range(nlev + 1):
                if l == nlev - 1:
                    query_row = odd_row if d == 0 else jnp.logical_not(odd_row)
                    G = jnp.exp(jnp.where(query_row, logf[:, sl], 0.0))
                    QG = (Qp * G).astype(BF16)
                    KG = Kp * G
                elif l < nlev:
                    G = jnp.exp(E[C * l:C * (l + 1), sl])
                    QG = (Qp * G).astype(BF16)
                    KG = Kp * G
                else:
                    QG = Qp.astype(BF16)
                    KG = Kp
                A = A + mask_ref[d, l] * _dot_nt(QG, by_head(KG))
            o = _dot(A.astype(BF16), by_head(Vp))
            eb = jnp.exp(E[C * (nlev - 1):C * nlev, sl])
            st = st_ref[d, p]
            o = o + _dot_nt((Qp * eb).astype(BF16), st.astype(BF16))
            o_ref[:, sl] = o
            Kd = (Kp * jnp.exp(E[C * nlev:C * (nlev + 1), sl])).astype(BF16)
            row = C - 1 if d == 0 else 0
            g = eb[row:row + 1, :]
            upd = _dot_tn(Vp.astype(BF16), Kd)
            st_ref[d, p] = st * g + jnp.where(same_head, upd, 0.0)


def hgrn2(b_all, lb, B, S):
    T = B * S
    C = CHUNK
    nc = S // C
    sums, masks = _hgrn_constants()
    sums = jnp.asarray(sums, BF16)
    masks = jnp.asarray(masks, F32)
    fwd = lambda col: pl.BlockSpec((C, BR_W), lambda b, n: (b * nc + n, col))
    bwd = lambda col: pl.BlockSpec((C, BR_W), lambda b, n: (b * nc + nc - 1 - n, col))
    full = lambda shape: pl.BlockSpec(shape, lambda b, n: (0,) * len(shape))
    return pl.pallas_call(
        _hgrn_kernel,
        grid=(B, nc),
        in_specs=[fwd(0), fwd(1), fwd(3), bwd(0), bwd(2), bwd(3),
                  full((2, 1, BR_W)), full(sums.shape), full(masks.shape)],
        out_specs=[pl.BlockSpec((C, BR_W), lambda b, n: (b * nc + n, 0)),
                   pl.BlockSpec((C, BR_W), lambda b, n: (b * nc + nc - 1 - n, 0))],
        out_shape=[jax.ShapeDtypeStruct((T, BR_W), F32), jax.ShapeDtypeStruct((T, BR_W), F32)],
        scratch_shapes=[pltpu.VMEM((2, B_HEADS // 2, 2 * B_DK, 2 * B_DK), F32)],
        compiler_params=_cparams(("parallel", "arbitrary")),
        name="hgrn2",
    )(b_all, b_all, b_all, b_all, b_all, b_all, lb, sums, masks)


HALO = 8


def _rglru_kernel(xf_ref, xfp_ref, xfn_ref, xb_ref, xbp_ref, xbn_ref, cw_ref, cb_ref, wa_ref, ba_ref,
                  wx_ref, bx_ref, sp_ref, hf_ref, hb_ref, carry_ref, *, tc):
    n = pl.program_id(1)
    nc = pl.num_programs(1)

    @pl.when(n == 0)
    def _():
        carry_ref[...] = jnp.zeros(carry_ref.shape, F32)

    row = lax.broadcasted_iota(jnp.int32, (tc, 1), 0)
    for d, (x_ref, xp_ref, xn_ref, o_ref) in enumerate(((xf_ref, xfp_ref, xfn_ref, hf_ref),
                                                        (xb_ref, xbp_ref, xbn_ref, hb_ref))):
        cidx = n if d == 0 else nc - 1 - n
        prev = jnp.where(cidx > 0, xp_ref[...], 0.0)
        nxt = jnp.where(cidx < nc - 1, xn_ref[...], 0.0)
        ext = jnp.concatenate([prev, x_ref[...], nxt], axis=0)
        xc = cb_ref[...]
        for j in range(CONV_W):
            off = HALO - CONV_LEFT + j
            xc = xc + ext[off:off + tc, :] * cw_ref[j:j + 1, :]
        xcb = xc.astype(BF16)
        r = _sigmoid(_dot(xcb, wa_ref[d]) + ba_ref[d])
        i = _sigmoid(_dot(xcb, wx_ref[d]) + bx_ref[d])
        log_a = (-RG_C) * r * sp_ref[d]
        a = jnp.exp(log_a)
        u = jnp.sqrt(1.0 - jnp.exp(2.0 * log_a)) * (i * xc)
        sh = 1
        while sh < tc:
            if d == 0:
                valid = row >= sh
                amt = sh
            else:
                valid = row < tc - sh
                amt = tc - sh
            a_sh = jnp.where(valid, pltpu.roll(a, amt, 0), 1.0)
            u_sh = jnp.where(valid, pltpu.roll(u, amt, 0), 0.0)
            u = a * u_sh + u
            a = a * a_sh
            sh *= 2
        h = u + a * carry_ref[d]
        o_ref[...] = h
        last = tc - 1 if d == 0 else 0
        carry_ref[d] = h[last:last + 1, :]


def rglru(d_all, w, B, S):
    T = B * S
    tc = _pick(S, 512)
    nc = S // tc
    hb = tc // HALO
    nrow8 = T // HALO
    fc = lambda b, n: b * nc + n
    bc = lambda b, n: b * nc + nc - 1 - n
    cur = lambda f: pl.BlockSpec((tc, BR_W), lambda b, n: (f(b, n), 0))
    prv = lambda f: pl.BlockSpec((HALO, BR_W), lambda b, n: (jnp.maximum(f(b, n) * hb - 1, 0), 0))
    nxt = lambda f: pl.BlockSpec((HALO, BR_W), lambda b, n: (jnp.minimum((f(b, n) + 1) * hb, nrow8 - 1), 0))
    full = lambda shape: pl.BlockSpec(shape, lambda b, n: (0,) * len(shape))
    kern = functools.partial(_rglru_kernel, tc=tc)
    return pl.pallas_call(
        kern,
        grid=(B, nc),
        in_specs=[cur(fc), prv(fc), nxt(fc), cur(bc), prv(bc), nxt(bc),
                  full((CONV_W, BR_W)), full((1, BR_W)), full((2, BR_W, BR_W)), full((2, 1, BR_W)),
                  full((2, BR_W, BR_W)), full((2, 1, BR_W)), full((2, 1, BR_W))],
        out_specs=[cur(fc), cur(bc)],
        out_shape=[jax.ShapeDtypeStruct((T, BR_W), F32), jax.ShapeDtypeStruct((T, BR_W), F32)],
        scratch_shapes=[pltpu.VMEM((2, 1, BR_W), F32)],
        compiler_params=_cparams(("parallel", "arbitrary")),
        name="rglru",
    )(d_all, d_all, d_all, d_all, d_all, d_all, w["conv_w"], w["conv_b"], w["wa"], w["ba"], w["wx"],
      w["bx"], w["sp"])


def _layer_norm(z, g, b):
    mu = jnp.mean(z, axis=1, keepdims=True)
    zc = z - mu
    var = jnp.mean(zc * zc, axis=1, keepdims=True)
    return zc * lax.rsqrt(var + LN_EPS) * g + b


def _merge_kernel(x_ref, ya_ref, of_ref, ob_ref, bg_ref, yc_ref, hf_ref, hb_ref, dg_ref, wg_ref, wbr_ref,
                  wo_ref, lng_ref, lnb_ref, hg_ref, havg_ref, wr_ref, x1_ref, aff_ref):
    x = x_ref[...]
    xb = x.astype(BF16)
    o = of_ref[...] + ob_ref[...]
    ms = _dot_split(o * o, havg_ref[...])
    bg = bg_ref[...]
    yb = o * lax.rsqrt(ms + RMS_EPS) * hg_ref[...] * (bg * _sigmoid(bg))
    dg = dg_ref[...]
    gelu = 0.5 * dg * (1.0 + jnp.tanh(0.7978845608028654 * (dg + 0.044715 * dg * dg * dg)))
    yd = (hf_ref[...] + hb_ref[...]) * gelu
    ys = (ya_ref[...], yb.astype(BF16), yc_ref[...], yd.astype(BF16))
    mix = None
    for i in range(N_BRANCH):
        gate = _sigmoid(_dot(xb, wg_ref[:, D_MODEL * i:D_MODEL * (i + 1)]))
        term = gate * _dot(ys[i], wbr_ref[i])
        mix = term if mix is None else mix + term
    z = ALPHA * x + _dot(mix.astype(BF16), wo_ref[...])
    x1 = _layer_norm(z, lng_ref[...], lnb_ref[...])
    x1_ref[...] = x1
    x1h = x1.astype(BF16)
    x1l = (x1 - x1h.astype(F32)).astype(BF16)
    logits = _dot_nt(x1h, wr_ref[0]) + (_dot_nt(x1l, wr_ref[0]) + _dot_nt(x1h, wr_ref[1]))
    e = jnp.exp(logits - jnp.max(logits, axis=1, keepdims=True))
    aff_ref[...] = e / jnp.sum(e, axis=1, keepdims=True)


def merge(x2, ya, of, ob, b_all, yc, hf, hb, d_all, w):
    T = x2.shape[0]
    tm = _pick(T, 512)
    full = lambda shape: pl.BlockSpec(shape, lambda i: (0,) * len(shape))
    rows = lambda wdt: pl.BlockSpec((tm, wdt), lambda i: (i, 0))
    colblk = lambda c: pl.BlockSpec((tm, BR_W), lambda i: (i, c))
    return pl.pallas_call(
        _merge_kernel,
        grid=(T // tm,),
        in_specs=[rows(D_MODEL), rows(BR_W), rows(BR_W), rows(BR_W), colblk(4), rows(BR_W), rows(BR_W),
                  rows(BR_W), colblk(1), full((D_MODEL, N_BRANCH * D_MODEL)),
                  full((N_BRANCH, BR_W, D_MODEL)), full((D_MODEL, D_MODEL)), full((1, D_MODEL)),
                  full((1, D_MODEL)), full((1, BR_W)), full((BR_W, BR_W)), full((2, N_EXPERTS, D_MODEL))],
        out_specs=[rows(D_MODEL), rows(N_EXPERTS)],
        out_shape=[jax.ShapeDtypeStruct((T, D_MODEL), F32), jax.ShapeDtypeStruct((T, N_EXPERTS), F32)],
        compiler_params=_cparams(("parallel",)),
        name="merge",
    )(x2, ya, of, ob, b_all, yc, hf, hb, d_all, w["gate"], w["branch"], w["out"], w["ln_g"], w["ln_b"],
      w["hgrn_norm_g"], w["head_avg"], w["router"])


def _select_kernel(aff_ref, sel_ref, *, cap, n_tok):
    aff = aff_ref[...]
    v = lax.bitcast_convert_type(aff, jnp.int32)
    E, R, L = aff.shape

    def count(pred):
        c = jnp.sum(pred.astype(F32), axis=1)
        return jnp.sum(c, axis=1, keepdims=True)

    def value_step(i, thr):
        cand = thr | jnp.left_shift(jnp.int32(1), 30 - i)
        ok = count(v >= cand[:, :, None]) >= cap
        return jnp.where(ok, cand, thr)

    thr = lax.fori_loop(0, 31, value_step, jnp.zeros((E, 1), jnp.int32))
    thr3 = thr[:, :, None]
    gt = v > thr3
    eq = v == thr3
    need = cap - count(gt)
    tok = (lax.broadcasted_iota(jnp.int32, (E, R, L), 1) * L
           + lax.broadcasted_iota(jnp.int32, (E, R, L), 2))
    nbits = max(1, int(n_tok).bit_length())

    def index_step(i, bound):
        cand = bound | jnp.left_shift(jnp.int32(1), nbits - 1 - i)
        ok = count(eq & (tok < cand[:, :, None])) <= need
        return jnp.where(ok, cand, bound)

    bound = lax.fori_loop(0, nbits, index_step, jnp.zeros((E, 1), jnp.int32))
    sel = gt | (eq & (tok < bound[:, :, None]))
    sel_ref[...] = sel.astype(jnp.int32)


def select_tokens(affT3, cap, n_tok):
    E, R, L = affT3.shape
    kern = functools.partial(_select_kernel, cap=cap, n_tok=n_tok)
    return pl.pallas_call(
        kern,
        grid=(1,),
        in_specs=[pl.BlockSpec((E, R, L), lambda i: (0, 0, 0))],
        out_specs=pl.BlockSpec((E, R, L), lambda i: (0, 0, 0)),
        out_shape=jax.ShapeDtypeStruct((E, R, L), jnp.int32),
        compiler_params=_cparams(("arbitrary",)),
        name="select_tokens",
    )(affT3)


FFN_FCHUNK = 512


def _ffn_kernel(idx_hbm, x_hbm, g_ref, wg_ref, wu_ref, wd_ref, ye_ref, idx_smem, xbuf, idx_sem, row_sem,
                *, tm, n_blocks):
    pair = pl.program_id(0) * pl.num_programs(1) + pl.program_id(1)
    last = n_blocks - 1

    def idx_copy(b, s):
        return pltpu.make_async_copy(idx_hbm.at[b], idx_smem.at[s], idx_sem.at[s])

    def row_copy(src_row, s, j):
        return pltpu.make_async_copy(x_hbm.at[pl.ds(src_row, 1), :], xbuf.at[s, pl.ds(j, 1), :],
                                     row_sem.at[s])

    def wait_rows(s):
        pltpu.make_async_copy(x_hbm.at[pl.ds(0, tm), :], xbuf.at[s], row_sem.at[s]).wait()

    @pl.when(pair == 0)
    def _():
        first = idx_copy(0, 0)
        first.start()
        first.wait()

        def issue(j, carry):
            row_copy(idx_smem[0, j], 0, j).start()
            return carry

        lax.fori_loop(0, tm, issue, 0)
        idx_copy(jnp.minimum(1, last), 1).start()

    n_chunks = D_EXPERT // FFN_FCHUNK
    per_chunk = tm // n_chunks
    for slot in range(2):
        nslot = 1 - slot
        blk = 2 * pair + slot
        idx_copy(0, nslot).wait()
        wait_rows(slot)
        xb = xbuf[slot].astype(BF16)
        acc = jnp.zeros((tm, D_MODEL), F32)
        for c in range(n_chunks):
            for j in range(per_chunk * c, per_chunk * (c + 1)):
                row_copy(idx_smem[nslot, j], nslot, j).start()
            fs = slice(FFN_FCHUNK * c, FFN_FCHUNK * (c + 1))
            hg = _dot(xb, wg_ref[0, :, fs])
            hu = _dot(xb, wu_ref[0, :, fs])
            h = (hg * _sigmoid(hg) * hu).astype(BF16)
            acc = acc + _dot(h, wd_ref[0, fs, :])
        ye_ref[tm * slot:tm * (slot + 1), :] = acc * g_ref[tm * slot:tm * (slot + 1), :]
        idx_copy(jnp.minimum(blk + 2, last), slot).start()

    @pl.when(2 * pair + 1 == last)
    def _():
        wait_rows(0)
        idx_copy(0, 1).wait()


def expert_ffn(idx, x1, g, wg, wu, wd):
    E, cap = idx.shape
    tm = _pick(cap // 2, 512)
    R = cap // tm
    RP = R // 2
    kern = functools.partial(_ffn_kernel, tm=tm, n_blocks=E * R)
    return pl.pallas_call(
        kern,
        grid=(E, RP),
        in_specs=[pl.BlockSpec(memory_space=pl.ANY),
                  pl.BlockSpec(memory_space=pl.ANY),
                  pl.BlockSpec((2 * tm, 1), lambda e, r: (e * RP + r, 0)),
                  pl.BlockSpec((1, D_MODEL, D_EXPERT), lambda e, r: (e, 0, 0)),
                  pl.BlockSpec((1, D_MODEL, D_EXPERT), lambda e, r: (e, 0, 0)),
                  pl.BlockSpec((1, D_EXPERT, D_MODEL), lambda e, r: (e, 0, 0))],
        out_specs=pl.BlockSpec((2 * tm, D_MODEL), lambda e, r: (e * RP + r, 0)),
        out_shape=jax.ShapeDtypeStruct((E * cap, D_MODEL), F32),
        scratch_shapes=[pltpu.SMEM((2, tm), jnp.int32),
                        pltpu.VMEM((2, tm, D_MODEL), F32),
                        pltpu.SemaphoreType.DMA((2,)),
                        pltpu.SemaphoreType.DMA((2,))],
        compiler_params=pltpu.CompilerParams(dimension_semantics=("arbitrary", "arbitrary"),
                                             vmem_limit_bytes=VMEM_LIMIT, disable_bounds_checks=True),
        name="expert_ffn",
    )(idx.reshape(E * R, tm), x1, g.reshape(E * cap, 1), wg, wu, wd)


CMB_TT = 256
CMB_PIECE = 32
CMB_CHUNK = 8
SUBLANES = 8


def _combine_kernel(p_ref, ye_hbm, sel_ref, x_ref, tri_ref, g_ref, b_ref, o_ref,
                    buf, meta, nfill, pre_ref, oh_ref, acc_ref, sem, *, cap, n_rows, n_tiles, max_pieces):
    i = pl.program_id(0)
    slot = i % 2
    E = N_EXPERTS
    PC = CMB_PIECE

    def piece_copy(src_row, s, k):
        return pltpu.make_async_copy(ye_hbm.at[pl.ds(src_row, PC), :], buf.at[s, pl.ds(k * PC, PC), :],
                                     sem.at[s])

    def issue_tile(tile, s):
        def expert_body(e, fill):
            lo = p_ref[tile * E + e]
            s0 = e * cap + lo
            s1 = e * cap + p_ref[(tile + 1) * E + e]
            a0 = (s0 // SUBLANES) * SUBLANES

            def piece(q, fill):
                start = a0 + PC * q
                start_c = pl.multiple_of(jnp.minimum(start, n_rows - PC), SUBLANES)
                piece_copy(start_c, s, fill).start()
                meta[s, 0, fill] = e
                meta[s, 1, fill] = start_c - s0
                meta[s, 2, fill] = start - start_c
                return fill + 1

            n_pieces = jnp.where(s1 > s0, (s1 - a0 + PC - 1) // PC, 0)
            return lax.fori_loop(0, n_pieces, piece, fill)

        nfill[s] = lax.fori_loop(0, E, expert_body, 0)

    @pl.when(i == 0)
    def _():
        buf[...] = jnp.zeros(buf.shape, F32)

        def clear(k, carry):
            for s in range(2):
                for f in range(3):
                    meta[s, f, k] = 0
            return carry

        lax.fori_loop(0, max_pieces, clear, 0)
        issue_tile(0, 0)

    @pl.when(i + 1 < n_tiles)
    def _():
        issue_tile(i + 1, 1 - slot)

    fill = nfill[slot]

    def wait_piece(k, carry):
        piece_copy(0, slot, k).wait()
        return carry

    lax.fori_loop(0, fill, wait_piece, 0)

    selv = sel_ref[...]
    pre_ref[...] = _dot(selv.astype(BF16), tri_ref[...])
    acc_ref[...] = jnp.zeros(acc_ref.shape, F32)
    row = lax.broadcasted_iota(jnp.int32, (PC, 1), 0)

    def chunk(c, carry):
        for k in range(CMB_CHUNK):
            pidx = c * CMB_CHUNK + k
            live = pidx < fill
            e = jnp.where(live, meta[slot, 0, pidx], 0)
            rank = jnp.where((row >= meta[slot, 2, pidx]) & live, row + meta[slot, 1, pidx], -1).astype(F32)
            hit = (pre_ref[pl.ds(e, 1), :] == rank) & (sel_ref[pl.ds(e, 1), :] > 0.5)
            oh_ref[PC * k:PC * (k + 1), :] = jnp.where(hit, 1.0, 0.0).astype(BF16)
        rows = buf[slot, pl.ds(pl.multiple_of(c * (CMB_CHUNK * PC), CMB_CHUNK * PC), CMB_CHUNK * PC), :]
        hi = rows.astype(BF16)
        lo = (rows - hi.astype(F32)).astype(BF16)
        oh = oh_ref[...]
        acc_ref[...] += _dot_tn(oh, hi) + _dot_tn(oh, lo)
        return carry

    lax.fori_loop(0, (fill + CMB_CHUNK - 1) // CMB_CHUNK, chunk, 0)
    o_ref[...] = _layer_norm(ALPHA * x_ref[...] + acc_ref[...], g_ref[...], b_ref[...])


def combine_norm(ye, sel_f, tile_prefix, x1, g, b, cap):
    T = x1.shape[0]
    tt = _pick(T, CMB_TT)
    n_tiles = T // tt
    max_pieces = N_EXPERTS * (tt // CMB_PIECE + 1)
    max_pieces = ((max_pieces + CMB_CHUNK - 1) // CMB_CHUNK) * CMB_CHUNK
    tri = jnp.asarray(np.triu(np.ones((tt, tt), np.float32), 1), BF16)
    kern = functools.partial(_combine_kernel, cap=cap, n_rows=ye.shape[0], n_tiles=n_tiles,
                             max_pieces=max_pieces)
    grid_spec = pltpu.PrefetchScalarGridSpec(
        num_scalar_prefetch=1,
        grid=(n_tiles,),
        in_specs=[pl.BlockSpec(memory_space=pl.ANY),
                  pl.BlockSpec((N_EXPERTS, tt), lambda i, p: (0, i)),
                  pl.BlockSpec((tt, D_MODEL), lambda i, p: (i, 0)),
                  pl.BlockSpec((tt, tt), lambda i, p: (0, 0)),
                  pl.BlockSpec((1, D_MODEL), lambda i, p: (0, 0)),
                  pl.BlockSpec((1, D_MODEL), lambda i, p: (0, 0))],
        out_specs=pl.BlockSpec((tt, D_MODEL), lambda i, p: (i, 0)),
        scratch_shapes=[pltpu.VMEM((2, max_pieces * CMB_PIECE, D_MODEL), F32),
                        pltpu.SMEM((2, 3, max_pieces), jnp.int32),
                        pltpu.SMEM((2,), jnp.int32),
                        pltpu.VMEM((N_EXPERTS, tt), F32),
                        pltpu.VMEM((CMB_CHUNK * CMB_PIECE, tt), BF16),
                        pltpu.VMEM((tt, D_MODEL), F32),
                        pltpu.SemaphoreType.DMA((2,))])
    return pl.pallas_call(
        kern,
        grid_spec=grid_spec,
        out_shape=jax.ShapeDtypeStruct((T, D_MODEL), F32),
        compiler_params=pltpu.CompilerParams(dimension_semantics=("arbitrary",), vmem_limit_bytes=VMEM_LIMIT,
                                             disable_bounds_checks=True),
        name="combine_norm",
    )(tile_prefix, ye, sel_f, x1, tri, g, b)


def _rot_cols(w, start):
    half = C_ROPE // 2
    return jnp.concatenate([-w[:, start + half:start + C_ROPE], w[:, start:start + half]], axis=1)


def _block_diag(w):
    n, c, _ = w.shape
    out = jnp.zeros((n * c, n * c), w.dtype)
    for i in range(n):
        out = out.at[i * c:(i + 1) * c, i * c:(i + 1) * c].set(w[i])
    return out


def _prep_layer(l, p, lb_all):
    w_in = p["w_in"][l]
    o = 0
    aq = w_in[:, o:o + BR_W]; o += BR_W
    ak = w_in[:, o:o + BR_W]; o += BR_W
    av = w_in[:, o:o + BR_W]; o += BR_W
    wb = w_in[:, o:o + 5 * BR_W]; o += 5 * BR_W
    c0 = o
    wcq = w_in[:, o:o + C_Q_LORA]; o += C_Q_LORA
    wckv = w_in[:, o:o + C_KV_LORA]; o += C_KV_LORA
    wckr = w_in[:, o:o + C_ROPE]; o += C_ROPE
    wd = w_in[:, o:o + 2 * BR_W]; o += 2 * BR_W
    wgate = w_in[:, o:o + N_BRANCH * D_MODEL]
    used = C_Q_LORA + C_KV_LORA + 2 * C_ROPE
    wc = jnp.concatenate([wcq, wckv, wckr, _rot_cols(w_in, c0 + C_Q_LORA + C_KV_LORA),
                          jnp.zeros((D_MODEL, C_PAD_W - used), F32)], axis=1)
    padq = jnp.zeros((D_MODEL, A_QKW - A_DK), F32)
    padv = jnp.zeros((D_MODEL, V_AUG_W - A_DV), F32)
    aq_w = jnp.concatenate([t for p_ in range(A_NP) for t in (aq[:, A_DK * p_:A_DK * (p_ + 1)], padq)], axis=1)
    ak_w = jnp.concatenate([t for p_ in range(A_NP) for t in (ak[:, A_DK * p_:A_DK * (p_ + 1)], padq)], axis=1)
    av_w = jnp.concatenate([t for h in range(A_HEADS) for t in (av[:, A_DV * h:A_DV * (h + 1)], padv)], axis=1)
    proj = dict(aq=aq_w.astype(BF16), akT=ak_w.T.astype(BF16), av=av_w.astype(BF16), b=wb.astype(BF16),
                c=wc.astype(BF16), d=wd.astype(BF16))

    wuq = p["mla_w_uq"][l]
    wuq_rot = jnp.zeros_like(wuq)
    for h in range(C_HEADS):
        r0 = C_QK * h + C_NOPE
        wuq_rot = wuq_rot.at[:, r0:r0 + C_ROPE].set(_rot_cols(wuq, r0))
    wukv = p["mla_w_ukv"][l].reshape(C_KV_LORA, C_HEADS, C_NOPE + C_V)
    wuv = jnp.concatenate([wukv[:, :, C_NOPE:], jnp.zeros((C_KV_LORA, C_HEADS, V_AUG_W - C_V), F32)], axis=2)
    mla = dict(q_norm_g=p["mla_q_norm_g"][l][None, :], kv_norm_g=p["mla_kv_norm_g"][l][None, :],
               wuq=wuq.astype(BF16), wuq_rot=wuq_rot.astype(BF16),
               wukT=wukv[:, :, :C_NOPE].reshape(C_KV_LORA, C_HEADS * C_NOPE).T.astype(BF16),
               wuv=wuv.reshape(C_KV_LORA, C_HEADS * V_AUG_W).astype(BF16))

    rg = dict(conv_w=p["rg_conv_w"][l], conv_b=p["rg_conv_b"][l][None, :],
              wa=jnp.stack([_block_diag(p["rg_w_a"][l, d]) for d in range(2)]).astype(BF16),
              ba=p["rg_b_a"][l][:, None, :],
              wx=jnp.stack([_block_diag(p["rg_w_x"][l, d]) for d in range(2)]).astype(BF16),
              bx=p["rg_b_x"][l][:, None, :],
              sp=jax.nn.softplus(-p["rg_lambda"][l].astype(F32))[:, None, :])

    head = np.arange(BR_W) // B_DK
    wr_hi = p["w_router"][l].astype(BF16)
    mrg = dict(gate=wgate.astype(BF16), branch=p["w_branch"][l].astype(BF16), out=p["w_out"][l].astype(BF16),
               ln_g=p["ln_g"][l, 0][None, :], ln_b=p["ln_b"][l, 0][None, :],
               hgrn_norm_g=p["hgrn_norm_g"][l][None, :],
               head_avg=jnp.asarray((head[:, None] == head[None, :]).astype(np.float32) / B_DK, BF16),
               router=jnp.stack([wr_hi.T, (p["w_router"][l] - wr_hi.astype(F32)).astype(BF16).T]))

    lp = p["diff_lambda"][l].astype(F32)
    lambda_init = 0.8 - 0.6 * math.exp(-0.3 * l)
    lam = (jnp.exp(jnp.sum(lp[0] * lp[1])) - jnp.exp(jnp.sum(lp[2] * lp[3])) + lambda_init).reshape(1)
    return dict(proj=proj, mla=mla, rg=rg, mrg=mrg, lam=lam, subln_g=p["diff_subln_g"][l][None, :],
                lb=lb_all[l][:, None, :],
                ffn=(p["w_e_gate"][l].astype(BF16), p["w_e_up"][l].astype(BF16), p["w_e_down"][l].astype(BF16)),
                ln2_g=p["ln_g"][l, 1][None, :], ln2_b=p["ln_b"][l, 1][None, :])


def _rope_tables(S):
    half = C_ROPE // 2
    inv = ROPE_BASE ** (-jnp.arange(0, C_ROPE, 2, dtype=F32) / C_ROPE)
    ang = jnp.arange(S, dtype=F32)[:, None] * inv[None, :]
    cos, sin = jnp.cos(ang), jnp.sin(ang)
    cosk = jnp.concatenate([cos, cos], axis=1)
    sink = jnp.concatenate([sin, sin], axis=1)
    ones = jnp.ones((S, C_NOPE), F32)
    zeros = jnp.zeros((S, C_NOPE), F32)
    cosq = jnp.concatenate([jnp.concatenate([ones, cosk], axis=1)] * C_HEADS, axis=1)
    sinq = jnp.concatenate([jnp.concatenate([zeros, sink], axis=1)] * C_HEADS, axis=1)
    tq, tk = _att_tiles(S)
    pos = jnp.arange(S, dtype=jnp.int32)
    i_rel = (pos % tq).astype(F32)
    j_rel = pos % tk
    j_hi = (16 * (j_rel // 16)).astype(F32)
    j_lo = (j_rel % 16).astype(F32)
    one = jnp.ones((S,), F32)
    qcols = jnp.stack([i_rel, one, one], axis=1)
    qpad = jnp.zeros((S, A_QKW - A_DK - 3), F32)
    qblock = jnp.concatenate([jnp.zeros((S, A_DK), F32), qcols, qpad], axis=1)
    a_qpos = jnp.concatenate([qblock] * A_NP, axis=1)
    kblocks = []
    for p_ in range(A_NP):
        sl = ALIBI_SLOPES[p_ // 2]
        krows = jnp.stack([sl * one, -sl * j_hi, -sl * j_lo], axis=0)
        kblocks += [jnp.zeros((A_DK, S), F32), krows, jnp.zeros((A_QKW - A_DK - 3, S), F32)]
    a_kpos = jnp.concatenate(kblocks, axis=0)
    v_ones = jnp.zeros((1, A_HEADS * V_AUG_W), F32).at[0, A_DV::V_AUG_W].set(1.0)
    return dict(cosq=cosq, sinq=sinq, cosk=cosk, sink=sink, eye=jnp.eye(C_ROPE, dtype=BF16),
                a_qpos=a_qpos, a_kpos=a_kpos, v_ones=v_ones)


def _layer(x2, B, S, l, w, tabs):
    T = B * S
    aq, akT, av, b_all, c_all, d_all = in_proj(x2, B, S, w["proj"], tabs)
    ya = diff_attention(aq, akT, av, w["lam"], w["subln_g"], l, B, S)
    of, ob = hgrn2(b_all, w["lb"], B, S)
    cq, ckT, cv = mla_prep(c_all, w["mla"], tabs, B, S)
    yc = mla_attention(cq, ckT, cv, B, S)
    hf, hb = rglru(d_all, w["rg"], B, S)
    x1, aff = merge(x2, ya, of, ob, b_all, yc, hf, hb, d_all, w["mrg"])

    cap = CAPACITY_FACTOR * T // N_EXPERTS
    affT = aff.T
    sel = select_tokens(affT.reshape(N_EXPERTS, T // 128, 128), cap, T).reshape(N_EXPERTS, T)
    tok = lax.broadcasted_iota(jnp.int32, sel.shape, 1)
    order = jnp.sort(jnp.where(sel > 0, tok, tok + T), axis=1)[:, :cap]
    g = jnp.take_along_axis(affT, order, axis=1)
    ye = expert_ffn(order, x1, g, *w["ffn"])
    tt = _pick(T, CMB_TT)
    counts = jnp.sum(sel.reshape(N_EXPERTS, T // tt, tt), axis=2)
    tile_prefix = jnp.concatenate([jnp.zeros((N_EXPERTS, 1), jnp.int32), jnp.cumsum(counts, axis=1)], axis=1)
    return combine_norm(ye, sel.astype(F32), tile_prefix.T.reshape(-1).astype(jnp.int32), x1,
                        w["ln2_g"], w["ln2_b"], cap)


def _trunk(x, weights):
    B, S, _ = x.shape
    tabs = _rope_tables(S)
    x2 = x.reshape(B * S, D_MODEL)
    for l in range(DEPTH):
        x2 = _layer(x2, B, S, l, weights[l], tabs)
    return x2.reshape(B, S, D_MODEL)


def kernel(x_prompt, x_sample, w_in, diff_lambda, diff_subln_g, hgrn_lb_logits, hgrn_norm_g, mla_q_norm_g,
           mla_w_uq, mla_kv_norm_g, mla_w_ukv, rg_conv_w, rg_conv_b, rg_w_a, rg_b_a, rg_w_x, rg_b_x, rg_lambda,
           w_branch, w_out, ln_g, ln_b, w_router, w_e_gate, w_e_up, w_e_down):
    lb_all = jnp.cumsum(jax.nn.softmax(hgrn_lb_logits.astype(F32), axis=0), axis=0)
    lb_all = lb_all - lb_all[:1]
    p = dict(w_in=w_in, diff_lambda=diff_lambda, diff_subln_g=diff_subln_g, hgrn_norm_g=hgrn_norm_g,
             mla_q_norm_g=mla_q_norm_g, mla_w_uq=mla_w_uq, mla_kv_norm_g=mla_kv_norm_g, mla_w_ukv=mla_w_ukv,
             rg_conv_w=rg_conv_w, rg_conv_b=rg_conv_b, rg_w_a=rg_w_a, rg_b_a=rg_b_a, rg_w_x=rg_w_x,
             rg_b_x=rg_b_x, rg_lambda=rg_lambda, w_branch=w_branch, w_out=w_out, ln_g=ln_g, ln_b=ln_b,
             w_router=w_router, w_e_gate=w_e_gate, w_e_up=w_e_up, w_e_down=w_e_down)
    weights = [_prep_layer(l, p, lb_all) for l in range(DEPTH)]
    return (_trunk(x_prompt, weights), _trunk(x_sample, weights))
```
